```python
import jax, jax.numpy as jnp
from jax import lax
import numpy as np

D_MODEL = 1024
BATCH = 8
SEQ = 2048
DEPTH = 4
DEC_BATCH = 128
DEC_SEQ = 1
PAST_LEN = 16384
PAGE_SIZE = 128

N_MEM = 256
CHUNK = 128
GMLP_GROUPS = 4
GMLP_HALF = D_MODEL // 2
GMLP_GROUP_W = GMLP_HALF // GMLP_GROUPS
CONV_B_W = D_MODEL // 2
CONV_B_K = 31
CONV_C_W = D_MODEL // 2
CONV_C_K = 3
MEM_HEADS = 4
MEM_HEAD_DIM = 128
MEM_W = MEM_HEADS * MEM_HEAD_DIM
D_FF = 2816
N_BRANCH = 4
EPS = 1e-6
OFF_A = 2 * GMLP_HALF
OFF_B = OFF_A + 2 * CONV_B_W
OFF_C = OFF_B + 3 * CONV_C_W
IN_COLS = OFF_C + MEM_W

kernel_name = "hybrid_gmlp_conformer_shortconv_memory_decoder_step"


def rms_norm(x, g):
    xf = x.astype(jnp.float32)
    y = xf * lax.rsqrt(jnp.mean(xf * xf, axis=-1, keepdims=True) + EPS)
    return (y * g.astype(jnp.float32)).astype(x.dtype)


def layer_norm(x, g, b):
    xf = x.astype(jnp.float32)
    mu = jnp.mean(xf, axis=-1, keepdims=True)
    var = jnp.mean(jnp.square(xf - mu), axis=-1, keepdims=True)
    y = (xf - mu) * lax.rsqrt(var + EPS)
    return (y * g.astype(jnp.float32) + b.astype(jnp.float32)).astype(x.dtype)


def swiglu_ffn(h, w_gate_up, w_down):
    g, u = jnp.split(h @ w_gate_up, 2, axis=-1)
    return (jax.nn.silu(g) * u) @ w_down


def chunk_spatial_mix(v, w_s, b_s):
    bsz, t, _ = v.shape
    n_chunks = -(-t // CHUNK)
    vp = jnp.pad(v, ((0, 0), (0, n_chunks * CHUNK - t), (0, 0)))
    vc = vp.reshape(bsz, n_chunks, CHUNK, GMLP_GROUPS, GMLP_GROUP_W)
    mask = jnp.tril(jnp.ones((CHUNK, CHUNK), dtype=bool))
    ws = jnp.where(mask, w_s, 0).astype(v.dtype)
    out = jnp.einsum('gij,bcjgd->bcigd', ws, vc) + jnp.transpose(b_s).astype(v.dtype)[:, :, None]
    return out.reshape(bsz, n_chunks * CHUNK, GMLP_HALF)[:, :t]


def causal_depthwise_conv(x, buf, w):
    k = w.shape[0]
    xc = jnp.concatenate([buf.astype(x.dtype), x], axis=1)
    y = lax.conv_general_dilated(xc, w[:, None, :].astype(x.dtype), window_strides=(1,),
                                 padding='VALID', dimension_numbers=('NWC', 'WIO', 'NWC'),
                                 feature_group_count=x.shape[-1])
    return y, xc[:, -(k - 1):]


def memory_kv(mem, g, w_k, w_v):
    bsz = mem.shape[0]
    m = rms_norm(mem, g)
    k = (m @ w_k).reshape(bsz, N_MEM, MEM_HEADS, MEM_HEAD_DIM)
    v = (m @ w_v).reshape(bsz, N_MEM, MEM_HEADS, MEM_HEAD_DIM)
    return k, v


def memory_attention(q, k, v):
    bsz, t = q.shape[0], q.shape[1]
    s = jnp.einsum('bthd,bmhd->bhtm', q, k.astype(q.dtype)).astype(jnp.float32) * (MEM_HEAD_DIM ** -0.5)
    p = jax.nn.softmax(s, axis=-1).astype(q.dtype)
    o = jnp.einsum('bhtm,bmhd->bthd', p, v.astype(q.dtype))
    return o.reshape(bsz, t, MEM_W)


def trunk_layer(x, mem_k, mem_v, buf_b, buf_c, lp):
    bsz, t, _ = x.shape
    x = x + 0.5 * swiglu_ffn(rms_norm(x, lp['ffn1_norm']), lp['ffn1_w_gate_up'], lp['ffn1_w_down'])
    h = rms_norm(x, lp['mix_norm'])
    z = h @ lp['w_in']
    z_a, z_b, z_c, z_q = z[..., :OFF_A], z[..., OFF_A:OFF_B], z[..., OFF_B:OFF_C], z[..., OFF_C:]
    u, v = jnp.split(jax.nn.gelu(z_a, approximate=False), 2, axis=-1)
    v = layer_norm(v, lp['gmlp_ln_g'], lp['gmlp_ln_b'])
    out_a = (u * chunk_spatial_mix(v, lp['gmlp_w_s'], lp['gmlp_b_s'])) @ lp['gmlp_w_out']
    b_val, b_gate = jnp.split(z_b, 2, axis=-1)
    y_b, new_buf_b = causal_depthwise_conv(b_val * jax.nn.sigmoid(b_gate), buf_b, lp['conv_b_w'])
    y_b = jax.nn.silu(layer_norm(y_b + lp['conv_b_bias'], lp['conv_b_ln_g'], lp['conv_b_ln_b']))
    out_b = y_b @ lp['conv_b_w_out']
    c_gate_b, c_gate_c, c_in = jnp.split(z_c, 3, axis=-1)
    y_c, new_buf_c = causal_depthwise_conv(c_gate_c * c_in, buf_c, lp['conv_c_w'])
    out_c = (c_gate_b * y_c) @ lp['conv_c_w_out']
    q = z_q.reshape(bsz, t, MEM_HEADS, MEM_HEAD_DIM)
    out_m = memory_attention(q, mem_k, mem_v) @ lp['mem_w_out']
    gates = jax.nn.sigmoid(h @ lp['w_branch_gate'] + lp['b_branch_gate']).reshape(bsz, t, N_BRANCH, D_MODEL)
    merged = (gates[:, :, 0] * out_a + gates[:, :, 1] * out_b
              + gates[:, :, 2] * out_c + gates[:, :, 3] * out_m)
    x = x + merged @ lp['w_o']
    x = x + 0.5 * swiglu_ffn(rms_norm(x, lp['ffn2_norm']), lp['ffn2_w_gate_up'], lp['ffn2_w_down'])
    return x, v, new_buf_b, new_buf_c


def setup_inputs(seed: int = 0) -> dict:
    key = jax.random.key(seed)
    ks = iter(jax.random.split(key, 48))
    f32 = jnp.float32
    L, D = DEPTH, D_MODEL

    def normal(shape, scale):
        return jax.random.normal(next(ks), shape, f32) * scale

    def gain(shape):
        return 1.0 + normal(shape, 0.02)

    return {
        'x_prompt': normal((BATCH, SEQ, D), 1.0),
        'x_sample': normal((DEC_BATCH, DEC_SEQ, D), 1.0),
        'mem_prompt': normal((BATCH, N_MEM, D), 1.0),
        'state_conv_b': normal((L, DEC_BATCH, CONV_B_K - 1, CONV_B_W), 0.5),
        'state_conv_c': normal((L, DEC_BATCH, CONV_C_K - 1, CONV_C_W), 0.5),
        'cache_mem_k': normal((L, DEC_BATCH, N_MEM, MEM_HEADS, MEM_HEAD_DIM), 1.0),
        'cache_mem_v': normal((L, DEC_BATCH, N_MEM, MEM_HEADS, MEM_HEAD_DIM), 1.0),
        'ffn1_norm': gain((L, D)),
        'ffn1_w_gate_up': normal((L, D, 2 * D_FF), D ** -0.5),
        'ffn1_w_down': normal((L, D_FF, D), D_FF ** -0.5),
        'mix_norm': gain((L, D)),
        'w_in': normal((L, D, IN_COLS), D ** -0.5),
        'gmlp_ln_g': gain((L, GMLP_HALF)),
        'gmlp_ln_b': normal((L, GMLP_HALF), 0.02),
        'gmlp_w_s': normal((L, GMLP_GROUPS, CHUNK, CHUNK), CHUNK ** -0.5),
        'gmlp_b_s': gain((L, GMLP_GROUPS, CHUNK)),
        'gmlp_w_out': normal((L, GMLP_HALF, D), GMLP_HALF ** -0.5),
        'conv_b_w': normal((L, CONV_B_K, CONV_B_W), CONV_B_K ** -0.5),
        'conv_b_bias': normal((L, CONV_B_W), 0.02),
        'conv_b_ln_g': gain((L, CONV_B_W)),
        'conv_b_ln_b': normal((L, CONV_B_W), 0.02),
        'conv_b_w_out': normal((L, CONV_B_W, D), CONV_B_W ** -0.5),
        'conv_c_w': normal((L, CONV_C_K, CONV_C_W), CONV_C_K ** -0.5),
        'conv_c_w_out': normal((L, CONV_C_W, D), CONV_C_W ** -0.5),
        'mem_norm': gain((L, D)),
        'mem_w_k': normal((L, D, MEM_W), D ** -0.5),
        'mem_w_v': normal((L, D, MEM_W), D ** -0.5),
        'mem_w_out': normal((L, MEM_W, D), MEM_W ** -0.5),
        'w_branch_gate': normal((L, D, N_BRANCH * D), D ** -0.5),
        'b_branch_gate': normal((L, N_BRANCH * D), 0.02),
        'w_o': normal((L, D, D), D ** -0.5),
        'ffn2_norm': gain((L, D)),
        'ffn2_w_gate_up': normal((L, D, 2 * D_FF), D ** -0.5),
        'ffn2_w_down': normal((L, D_FF, D), D_FF ** -0.5),
        'final_norm': gain((D,)),
    }


def reference(x_prompt, x_sample, mem_prompt, state_conv_b, state_conv_c, cache_mem_k, cache_mem_v,
              ffn1_norm, ffn1_w_gate_up, ffn1_w_down, mix_norm, w_in,
              gmlp_ln_g, gmlp_ln_b, gmlp_w_s, gmlp_b_s, gmlp_w_out,
              conv_b_w, conv_b_bias, conv_b_ln_g, conv_b_ln_b, conv_b_w_out,
              conv_c_w, conv_c_w_out, mem_norm, mem_w_k, mem_w_v, mem_w_out,
              w_branch_gate, b_branch_gate, w_o, ffn2_norm, ffn2_w_gate_up, ffn2_w_down, final_norm):
    bp = x_prompt.shape[0]
    zero_buf_b = jnp.zeros((bp, CONV_B_K - 1, CONV_B_W), x_prompt.dtype)
    zero_buf_c = jnp.zeros((bp, CONV_C_K - 1, CONV_C_W), x_prompt.dtype)
    xp, xs = x_prompt, x_sample
    mk_p, mv_p, cb_p, cb_s, cc_p, cc_s, gv_s = [], [], [], [], [], [], []
    for l in range(DEPTH):
        lp = {
            'ffn1_norm': ffn1_norm[l], 'ffn1_w_gate_up': ffn1_w_gate_up[l], 'ffn1_w_down': ffn1_w_down[l],
            'mix_norm': mix_norm[l], 'w_in': w_in[l],
            'gmlp_ln_g': gmlp_ln_g[l], 'gmlp_ln_b': gmlp_ln_b[l], 'gmlp_w_s': gmlp_w_s[l],
            'gmlp_b_s': gmlp_b_s[l], 'gmlp_w_out': gmlp_w_out[l],
            'conv_b_w': conv_b_w[l], 'conv_b_bias': conv_b_bias[l], 'conv_b_ln_g': conv_b_ln_g[l],
            'conv_b_ln_b': conv_b_ln_b[l], 'conv_b_w_out': conv_b_w_out[l],
            'conv_c_w': conv_c_w[l], 'conv_c_w_out': conv_c_w_out[l], 'mem_w_out': mem_w_out[l],
            'w_branch_gate': w_branch_gate[l], 'b_branch_gate': b_branch_gate[l], 'w_o': w_o[l],
            'ffn2_norm': ffn2_norm[l], 'ffn2_w_gate_up': ffn2_w_gate_up[l], 'ffn2_w_down': ffn2_w_down[l],
        }
        k_p, v_p = memory_kv(mem_prompt, mem_norm[l], mem_w_k[l], mem_w_v[l])
        xp, _, nb_p, nc_p = trunk_layer(xp, k_p, v_p, zero_buf_b, zero_buf_c, lp)
        xs, v_s, nb_s, nc_s = trunk_layer(xs, cache_mem_k[l], cache_mem_v[l],
                                          state_conv_b[l], state_conv_c[l], lp)
        mk_p.append(k_p); mv_p.append(v_p)
        cb_p.append(nb_p); cb_s.append(nb_s)
        cc_p.append(nc_p); cc_s.append(nc_s)
        gv_s.append(v_s)
    y_prompt = rms_norm(xp, final_norm)
    y_sample = rms_norm(xs, final_norm)
    return (y_prompt, y_sample, jnp.stack(mk_p), jnp.stack(mv_p), jnp.stack(cb_p), jnp.stack(cb_s),
            jnp.stack(cc_p), jnp.stack(cc_s), jnp.stack(gv_s))
```

```python
import functools

import jax
import jax.numpy as jnp
from jax import lax
from jax.experimental import pallas as pl
from jax.experimental.pallas import tpu as pltpu

F32 = jnp.float32
BF16 = jnp.bfloat16

D_MODEL = 1024
DEPTH = 4
N_MEM = 256
CHUNK = 128
GROUPS = 4
HALF = 512
CONV_B_K = 31
CONV_C_K = 3
HEADS = 4
HEAD_DIM = 128
D_FF = 2816
EPS = 1e-6
ATTN_SCALE = HEAD_DIM ** -0.5

SUBLANES = 8
LANES = 128

FF_CHUNK = 256
N_FF_CHUNKS = D_FF // FF_CHUNK
TM_FFN = 512
TM_MIX = 512
CONV_ROWS = 32
HALO_B = 32
HALO_C = 8
SAMPLE_BLOCK = 8
VMEM_LIMIT = 56 * 1024 * 1024


def _dot(a, b):
    return jnp.dot(a, b, preferred_element_type=F32)


def _rms(x, g):
    y = x * lax.rsqrt(jnp.mean(x * x, axis=-1, keepdims=True) + EPS)
    return y * g


def _gelu(x):
    return 0.5 * x * (1.0 + lax.erf(x * (0.5 ** 0.5)))


def _layer_norm(x, g, b):
    mu = jnp.mean(x, axis=-1, keepdims=True)
    xc = x - mu
    var = jnp.mean(xc * xc, axis=-1, keepdims=True)
    return xc * lax.rsqrt(var + EPS) * g + b


def _ffn_kernel(*refs, final):
    if final:
        x_ref, g_ref, wgu_ref, wd_ref, fg_ref, o_ref = refs
    else:
        x_ref, g_ref, wgu_ref, wd_ref, o_ref = refs
    x = x_ref[...]
    h = _rms(x, g_ref[...]).astype(BF16)
    acc = None
    for c in range(N_FF_CHUNKS):
        gu = _dot(h, wgu_ref[:, c * 2 * FF_CHUNK:(c + 1) * 2 * FF_CHUNK])
        a = (jax.nn.silu(gu[:, :FF_CHUNK]) * gu[:, FF_CHUNK:]).astype(BF16)
        d = _dot(a, wd_ref[c * FF_CHUNK:(c + 1) * FF_CHUNK, :])
        acc = d if acc is None else acc + d
    y = x + 0.5 * acc
    if final:
        y = _rms(y, fg_ref[...])
    o_ref[...] = y


def _resident(shape, layer):
    nd = len(shape)
    return pl.BlockSpec((None,) + tuple(shape), lambda *_: (layer,) + (0,) * nd,
                        pipeline_mode=pl.Buffered(1))


def _ffn_call(x, layer, norm, wgu, wd, final_norm, *, tm, name):
    m = x.shape[0]
    final = final_norm is not None
    in_specs = [
        pl.BlockSpec((tm, D_MODEL), lambda i: (i, 0)),
        _resident((1, D_MODEL), layer),
        _resident((D_MODEL, 2 * D_FF), layer),
        _resident((D_FF, D_MODEL), layer),
    ]
    args = [x, norm, wgu, wd]
    if final:
        in_specs.append(pl.BlockSpec((1, D_MODEL), lambda i: (0, 0)))
        args.append(final_norm)
    return pl.pallas_call(
        functools.partial(_ffn_kernel, final=final),
        grid=(m // tm,),
        in_specs=in_specs,
        out_specs=pl.BlockSpec((tm, D_MODEL), lambda i: (i, 0)),
        out_shape=jax.ShapeDtypeStruct((m, D_MODEL), F32),
        compiler_params=pltpu.CompilerParams(
            dimension_semantics=("arbitrary",), vmem_limit_bytes=VMEM_LIMIT),
        name=name,
    )(*args)


def _memkv_kernel(m_ref, g_ref, wk_ref, wv_ref, k_ref, v_ref):
    m = _rms(m_ref[...], g_ref[...]).astype(BF16)
    k_ref[...] = _dot(m, wk_ref[...])
    v_ref[...] = _dot(m, wv_ref[...])


def _memkv_call(mem, norm, wk, wv):
    rows = mem.shape[0]
    tm = 512
    per_layer = lambda shape: pl.BlockSpec((None,) + shape, lambda l, i: (l, 0, 0))
    out_spec = pl.BlockSpec((None, tm, HALF), lambda l, i: (l, i, 0))
    return pl.pallas_call(
        _memkv_kernel,
        grid=(DEPTH, rows // tm),
        in_specs=[pl.BlockSpec((tm, D_MODEL), lambda l, i: (i, 0)),
                  per_layer((1, D_MODEL)), per_layer((D_MODEL, HALF)), per_layer((D_MODEL, HALF))],
        out_specs=[out_spec, out_spec],
        out_shape=[jax.ShapeDtypeStruct((DEPTH, rows, HALF), F32)] * 2,
        compiler_params=pltpu.CompilerParams(
            dimension_semantics=("arbitrary", "arbitrary"), vmem_limit_bytes=VMEM_LIMIT),
        name="memkv",
    )(mem, norm, wk, wv)


def _causal_conv(ext_ref, w_ref, taps, first_row, rows):
    outs = []
    for r in range(rows // CONV_ROWS):
        base = first_row + r * CONV_ROWS
        acc = None
        for k in range(taps):
            term = w_ref[k:k + 1, :] * ext_ref[base + k:base + k + CONV_ROWS, :]
            acc = term if acc is None else acc + term
        outs.append(acc)
    return jnp.concatenate(outs, axis=0)


def _mix_prompt_kernel(x_ref, k_ref, v_ref, norm_ref, win_ref, lng_ref, lnb_ref, ws_ref, bst_ref,
                       wa_ref, cbw_ref, cbb_ref, cblg_ref, cblb_ref, wb_ref, ccw_ref, wc_ref, wm_ref,
                       wg_ref, bg_ref, wo_ref,
                       xo_ref, tailb_ref, tailc_ref,
                       extb_ref, extc_ref):
    tm = x_ref.shape[0]
    j = pl.program_id(1)

    @pl.when(j == 0)
    def _():
        extb_ref[0:HALO_B, :] = jnp.zeros((HALO_B, HALF), F32)
        extc_ref[0:HALO_C, :] = jnp.zeros((HALO_C, HALF), F32)

    x = x_ref[...]
    h = _rms(x, norm_ref[...]).astype(BF16)

    def gate(i):
        z = _dot(h, wg_ref[:, i * D_MODEL:(i + 1) * D_MODEL]) + bg_ref[:, i * D_MODEL:(i + 1) * D_MODEL]
        return jax.nn.sigmoid(z)

    za = _gelu(_dot(h, win_ref[:, 0:2 * HALF]))
    u = za[:, :HALF]
    v = _layer_norm(za[:, HALF:], lng_ref[...], lnb_ref[...]).astype(BF16)
    n_chunks = tm // CHUNK
    row = lax.broadcasted_iota(jnp.int32, (CHUNK, CHUNK), 0)
    col = lax.broadcasted_iota(jnp.int32, (CHUNK, CHUNK), 1)
    pieces = [[None] * GROUPS for _ in range(n_chunks)]
    for g in range(GROUPS):
        ws = jnp.where(row >= col, ws_ref[g], 0.0).astype(BF16)
        rhs = jnp.concatenate(
            [v[c * CHUNK:(c + 1) * CHUNK, g * LANES:(g + 1) * LANES] for c in range(n_chunks)], axis=1)
        res = _dot(ws, rhs) + bst_ref[:, g:g + 1]
        for c in range(n_chunks):
            pieces[c][g] = res[:, c * CHUNK:(c + 1) * CHUNK]
    mixed = jnp.concatenate([jnp.concatenate(p, axis=1) for p in pieces], axis=0)
    merged = gate(0) * _dot((u * mixed).astype(BF16), wa_ref[...])

    zb = _dot(h, win_ref[:, 2 * HALF:4 * HALF])
    extb_ref[HALO_B:HALO_B + tm, :] = zb[:, :HALF] * jax.nn.sigmoid(zb[:, HALF:])
    yb = _causal_conv(extb_ref, cbw_ref, CONV_B_K, HALO_B - (CONV_B_K - 1), tm) + cbb_ref[...]
    yb = jax.nn.silu(_layer_norm(yb, cblg_ref[...], cblb_ref[...]))
    merged = merged + gate(1) * _dot(yb.astype(BF16), wb_ref[...])

    zc = _dot(h, win_ref[:, 4 * HALF:7 * HALF])
    extc_ref[HALO_C:HALO_C + tm, :] = zc[:, HALF:2 * HALF] * zc[:, 2 * HALF:]
    yc = _causal_conv(extc_ref, ccw_ref, CONV_C_K, HALO_C - (CONV_C_K - 1), tm)
    merged = merged + gate(2) * _dot((zc[:, :HALF] * yc).astype(BF16), wc_ref[...])

    q = _dot(h, win_ref[:, 7 * HALF:8 * HALF])
    kb = k_ref[...].astype(BF16)
    vb = v_ref[...].astype(BF16)
    ones = jnp.ones((N_MEM, HEAD_DIM), BF16)
    heads = []
    for hd in range(HEADS):
        sl = slice(hd * HEAD_DIM, (hd + 1) * HEAD_DIM)
        s = lax.dot_general(q[:, sl].astype(BF16), kb[:, sl], (((1,), (1,)), ((), ())),
                            preferred_element_type=F32) * ATTN_SCALE
        e = jnp.exp(s - jnp.max(s, axis=-1, keepdims=True)).astype(BF16)
        r = _dot(e, jnp.concatenate([vb[:, sl], ones], axis=1))
        heads.append(r[:, :HEAD_DIM] / r[:, HEAD_DIM:])
    om = jnp.concatenate(heads, axis=1).astype(BF16)
    merged = merged + gate(3) * _dot(om, wm_ref[...])

    xo_ref[...] = x + _dot(merged.astype(BF16), wo_ref[...])

    @pl.when(j == pl.num_programs(1) - 1)
    def _():
        tailb_ref[...] = extb_ref[tm + HALO_B - (CONV_B_K - 1):tm + HALO_B, :]
        tailc_ref[...] = extc_ref[tm + HALO_C - (CONV_C_K - 1):tm + HALO_C, :]

    extb_ref[0:HALO_B, :] = extb_ref[tm:tm + HALO_B, :]
    extc_ref[0:HALO_C, :] = extc_ref[tm:tm + HALO_C, :]


def _mix_prompt_call(x, kmem, vmem_, layer, w, *, batch, seq):
    tm = TM_MIX
    nj = seq // tm
    in_specs = [
        pl.BlockSpec((tm, D_MODEL), lambda b, j: (b * nj + j, 0)),
        pl.BlockSpec((None, N_MEM, HALF), lambda b, j: (layer, b, 0)),
        pl.BlockSpec((None, N_MEM, HALF), lambda b, j: (layer, b, 0)),
        _resident((1, D_MODEL), layer),
        _resident((D_MODEL, 8 * HALF), layer),
        _resident((1, HALF), layer),
        _resident((1, HALF), layer),
        _resident((GROUPS, CHUNK, CHUNK), layer),
        _resident((CHUNK, GROUPS), layer),
        _resident((HALF, D_MODEL), layer),
        _resident((CONV_B_K, HALF), layer),
        _resident((1, HALF), layer),
        _resident((1, HALF), layer),
        _resident((1, HALF), layer),
        _resident((HALF, D_MODEL), layer),
        _resident((CONV_C_K, HALF), layer),
        _resident((HALF, D_MODEL), layer),
        _resident((HALF, D_MODEL), layer),
        _resident((D_MODEL, 4 * D_MODEL), layer),
        _resident((1, 4 * D_MODEL), layer),
        _resident((D_MODEL, D_MODEL), layer),
    ]
    out_specs = [
        pl.BlockSpec((tm, D_MODEL), lambda b, j: (b * nj + j, 0)),
        pl.BlockSpec((None, CONV_B_K - 1, HALF), lambda b, j: (b, 0, 0)),
        pl.BlockSpec((None, CONV_C_K - 1, HALF), lambda b, j: (b, 0, 0)),
    ]
    out_shape = [
        jax.ShapeDtypeStruct((batch * seq, D_MODEL), F32),
        jax.ShapeDtypeStruct((batch, CONV_B_K - 1, HALF), F32),
        jax.ShapeDtypeStruct((batch, CONV_C_K - 1, HALF), F32),
    ]
    return pl.pallas_call(
        _mix_prompt_kernel,
        grid=(batch, nj),
        in_specs=in_specs,
        out_specs=out_specs,
        out_shape=out_shape,
        scratch_shapes=[pltpu.VMEM((HALO_B + tm, HALF), F32), pltpu.VMEM((HALO_C + tm, HALF), F32)],
        compiler_params=pltpu.CompilerParams(
            dimension_semantics=("arbitrary", "arbitrary"), vmem_limit_bytes=VMEM_LIMIT),
        name=f"mix_prompt_{layer}",
    )(x, kmem, vmem_, w["mix_norm"], w["w_in"], w["gmlp_ln_g"], w["gmlp_ln_b"], w["gmlp_w_s"],
      w["gmlp_b_s_t"], w["gmlp_w_out"], w["conv_b_w"], w["conv_b_bias"], w["conv_b_ln_g"],
      w["conv_b_ln_b"], w["conv_b_w_out"], w["conv_c_w"], w["conv_c_w_out"], w["mem_w_out"],
      w["w_branch_gate"], w["b_branch_gate"], w["w_o"])


def _mix_sample_kernel(x_ref, k_ref, v_ref, stb_ref, stc_ref, norm_ref, win_ref, lng_ref, lnb_ref,
                       ws0_ref, bs0_ref, wa_ref, cbw_ref, cbb_ref, cblg_ref, cblb_ref, wb_ref, ccw_ref,
                       wc_ref, wm_ref, wg_ref, bg_ref, wo_ref,
                       xo_ref, gv_ref, nstb_ref, nstc_ref,
                       h_scr, pa_scr, xinb_scr, gatec_scr, xinc_scr, q_scr, yb_scr, yc_scr, o_scr):
    i = pl.program_id(0)
    sb = stb_ref.shape[0]
    r0 = pl.multiple_of(i * sb, sb)

    @pl.when(i == 0)
    def _():
        h = _rms(x_ref[...], norm_ref[...]).astype(BF16)
        h_scr[...] = h
        za = _gelu(_dot(h, win_ref[:, 0:2 * HALF]))
        v = _layer_norm(za[:, HALF:], lng_ref[...], lnb_ref[...])
        gv_ref[...] = v
        pa_scr[...] = za[:, :HALF] * (ws0_ref[...] * v + bs0_ref[...])
        zb = _dot(h, win_ref[:, 2 * HALF:4 * HALF])
        xinb_scr[...] = zb[:, :HALF] * jax.nn.sigmoid(zb[:, HALF:])
        zc = _dot(h, win_ref[:, 4 * HALF:7 * HALF])
        gatec_scr[...] = zc[:, :HALF]
        xinc_scr[...] = zc[:, HALF:2 * HALF] * zc[:, 2 * HALF:]
        q = _dot(h, win_ref[:, 7 * HALF:8 * HALF])
        for hd in range(HEADS):
            qh = q[:, hd * HEAD_DIM:(hd + 1) * HEAD_DIM]
            q_scr[:, hd, :] = qh
            q_scr[:, hd + HEADS, :] = qh

    xin_b = xinb_scr[pl.ds(r0, sb), :]
    acc = cbw_ref[CONV_B_K - 1:CONV_B_K, :] * xin_b
    for k in range(CONV_B_K - 1):
        acc = acc + cbw_ref[k:k + 1, :] * stb_ref[:, k, :]
    yb_scr[pl.ds(r0, sb), :] = acc
    for k in range(CONV_B_K - 2):
        nstb_ref[:, k, :] = stb_ref[:, k + 1, :]
    nstb_ref[:, CONV_B_K - 2, :] = xin_b

    xin_c = xinc_scr[pl.ds(r0, sb), :]
    yc_scr[pl.ds(r0, sb), :] = (ccw_ref[0:1, :] * stc_ref[:, 0, :] + ccw_ref[1:2, :] * stc_ref[:, 1, :]
                                + ccw_ref[2:3, :] * xin_c)
    nstc_ref[:, 0, :] = stc_ref[:, 1, :]
    nstc_ref[:, 1, :] = xin_c

    ones = jnp.ones((HEAD_DIM, LANES), BF16)
    n_tiles = N_MEM * HEADS // SUBLANES
    for s in range(sb):
        q8 = q_scr[r0 + s]
        prod = (k_ref[s] * q8[None]).reshape(n_tiles * SUBLANES, HEAD_DIM)
        sc = (_dot(prod.astype(BF16), ones) * ATTN_SCALE).reshape(n_tiles, SUBLANES, LANES)
        mx = jnp.max(sc, axis=0)
        mx = jnp.maximum(mx, pltpu.roll(mx, HEADS, axis=0))
        e = jnp.exp(sc - mx[None])
        num = jnp.sum(e * v_ref[s], axis=0)
        den = jnp.sum(e, axis=0)
        num = num + pltpu.roll(num, HEADS, axis=0)
        den = den + pltpu.roll(den, HEADS, axis=0)
        o_scr[r0 + s] = num / den

    @pl.when(i == pl.num_programs(0) - 1)
    def _():
        h = h_scr[...]

        def gate(n):
            z = _dot(h, wg_ref[:, n * D_MODEL:(n + 1) * D_MODEL]) + bg_ref[:, n * D_MODEL:(n + 1) * D_MODEL]
            return jax.nn.sigmoid(z)

        merged = gate(0) * _dot(pa_scr[...].astype(BF16), wa_ref[...])
        yb = jax.nn.silu(_layer_norm(yb_scr[...] + cbb_ref[...], cblg_ref[...], cblb_ref[...]))
        merged = merged + gate(1) * _dot(yb.astype(BF16), wb_ref[...])
        merged = merged + gate(2) * _dot((gatec_scr[...] * yc_scr[...]).astype(BF16), wc_ref[...])
        om = None
        for hd in range(HEADS):
            part = _dot(o_scr[:, hd, :].astype(BF16), wm_ref[hd * HEAD_DIM:(hd + 1) * HEAD_DIM, :])
            om = part if om is None else om + part
        merged = merged + gate(3) * om
        xo_ref[...] = x_ref[...] + _dot(merged.astype(BF16), wo_ref[...])


def _mix_sample_call(x, kc, vc, stb, stc, layer, w):
    n = x.shape[0]
    sb = SAMPLE_BLOCK
    n_tiles = N_MEM * HEADS // SUBLANES
    const = lambda shape: pl.BlockSpec(shape, lambda i: (0,) * len(shape))
    in_specs = [
        const((n, D_MODEL)),
        pl.BlockSpec((None, sb, n_tiles, SUBLANES, HEAD_DIM), lambda i: (layer, i, 0, 0, 0)),
        pl.BlockSpec((None, sb, n_tiles, SUBLANES, HEAD_DIM), lambda i: (layer, i, 0, 0, 0)),
        pl.BlockSpec((None, sb, CONV_B_K - 1, HALF), lambda i: (layer, i, 0, 0)),
        pl.BlockSpec((None, sb, CONV_C_K - 1, HALF), lambda i: (layer, i, 0, 0)),
        _resident((1, D_MODEL), layer),
        _resident((D_MODEL, 8 * HALF), layer),
        _resident((1, HALF), layer),
        _resident((1, HALF), layer),
        _resident((1, HALF), layer),
        _resident((1, HALF), layer),
        _resident((HALF, D_MODEL), layer),
        _resident((CONV_B_K, HALF), layer),
        _resident((1, HALF), layer),
        _resident((1, HALF), layer),
        _resident((1, HALF), layer),
        _resident((HALF, D_MODEL), layer),
        _resident((CONV_C_K, HALF), layer),
        _resident((HALF, D_MODEL), layer),
        _resident((HALF, D_MODEL), layer),
        _resident((D_MODEL, 4 * D_MODEL), layer),
        _resident((1, 4 * D_MODEL), layer),
        _resident((D_MODEL, D_MODEL), layer),
    ]
    out_specs = [
        const((n, D_MODEL)),
        const((n, HALF)),
        pl.BlockSpec((sb, CONV_B_K - 1, HALF), lambda i: (i, 0, 0)),
        pl.BlockSpec((sb, CONV_C_K - 1, HALF), lambda i: (i, 0, 0)),
    ]
    out_shape = [
        jax.ShapeDtypeStruct((n, D_MODEL), F32),
        jax.ShapeDtypeStruct((n, HALF), F32),
        jax.ShapeDtypeStruct((n, CONV_B_K - 1, HALF), F32),
        jax.ShapeDtypeStruct((n, CONV_C_K - 1, HALF), F32),
    ]
    scratch = [
        pltpu.VMEM((n, D_MODEL), BF16),
        pltpu.VMEM((n, HALF), F32),
        pltpu.VMEM((n, HALF), F32),
        pltpu.VMEM((n, HALF), F32),
        pltpu.VMEM((n, HALF), F32),
        pltpu.VMEM((n, SUBLANES, HEAD_DIM), F32),
        pltpu.VMEM((n, HALF), F32),
        pltpu.VMEM((n, HALF), F32),
        pltpu.VMEM((n, SUBLANES, HEAD_DIM), F32),
    ]
    return pl.pallas_call(
        _mix_sample_kernel,
        grid=(n // sb,),
        in_specs=in_specs,
        out_specs=out_specs,
        out_shape=out_shape,
        scratch_shapes=scratch,
        compiler_params=pltpu.CompilerParams(
            dimension_semantics=("arbitrary",), vmem_limit_bytes=VMEM_LIMIT),
        name=f"mix_sample_{layer}",
    )(x, kc, vc, stb, stc, w["mix_norm"], w["w_in"], w["gmlp_ln_g"], w["gmlp_ln_b"], w["gmlp_ws0"],
      w["gmlp_bs0"], w["gmlp_w_out"], w["conv_b_w"], w["conv_b_bias"], w["conv_b_ln_g"],
      w["conv_b_ln_b"], w["conv_b_w_out"], w["conv_c_w"], w["conv_c_w_out"], w["mem_w_out"],
      w["w_branch_gate"], w["b_branch_gate"], w["w_o"])


def _interleave_gate_up(w):
    gate = w[:, :, :D_FF].reshape(DEPTH, D_MODEL, N_FF_CHUNKS, FF_CHUNK)
    up = w[:, :, D_FF:].reshape(DEPTH, D_MODEL, N_FF_CHUNKS, FF_CHUNK)
    return jnp.concatenate([gate, up], axis=3).reshape(DEPTH, D_MODEL, 2 * D_FF).astype(BF16)


def kernel(x_prompt, x_sample, mem_prompt, state_conv_b, state_conv_c, cache_mem_k, cache_mem_v,
           ffn1_norm, ffn1_w_gate_up, ffn1_w_down, mix_norm, w_in,
           gmlp_ln_g, gmlp_ln_b, gmlp_w_s, gmlp_b_s, gmlp_w_out,
           conv_b_w, conv_b_bias, conv_b_ln_g, conv_b_ln_b, conv_b_w_out,
           conv_c_w, conv_c_w_out, mem_norm, mem_w_k, mem_w_v, mem_w_out,
           w_branch_gate, b_branch_gate, w_o, ffn2_norm, ffn2_w_gate_up, ffn2_w_down, final_norm):
    batch, seq, _ = x_prompt.shape
    n_sample = x_sample.shape[0]
    row = lambda a: a.reshape(DEPTH, 1, a.shape[-1])

    w = {
        "mix_norm": row(mix_norm), "w_in": w_in.astype(BF16),
        "gmlp_ln_g": row(gmlp_ln_g), "gmlp_ln_b": row(gmlp_ln_b),
        "gmlp_w_s": gmlp_w_s, "gmlp_b_s_t": jnp.swapaxes(gmlp_b_s, 1, 2),
        "gmlp_ws0": row(jnp.repeat(gmlp_w_s[:, :, 0, 0], LANES, axis=1)),
        "gmlp_bs0": row(jnp.repeat(gmlp_b_s[:, :, 0], LANES, axis=1)),
        "gmlp_w_out": gmlp_w_out.astype(BF16),
        "conv_b_w": conv_b_w, "conv_b_bias": row(conv_b_bias),
        "conv_b_ln_g": row(conv_b_ln_g), "conv_b_ln_b": row(conv_b_ln_b),
        "conv_b_w_out": conv_b_w_out.astype(BF16),
        "conv_c_w": conv_c_w, "conv_c_w_out": conv_c_w_out.astype(BF16),
        "mem_w_out": mem_w_out.astype(BF16),
        "w_branch_gate": w_branch_gate.astype(BF16), "b_branch_gate": row(b_branch_gate),
        "w_o": w_o.astype(BF16),
    }
    ffn1 = (row(ffn1_norm), _interleave_gate_up(ffn1_w_gate_up), ffn1_w_down.astype(BF16))
    ffn2 = (row(ffn2_norm), _interleave_gate_up(ffn2_w_gate_up), ffn2_w_down.astype(BF16))
    final = final_norm.reshape(1, D_MODEL)

    kmem, vmem_ = _memkv_call(mem_prompt.reshape(batch * N_MEM, D_MODEL), row(mem_norm),
                              mem_w_k.astype(BF16), mem_w_v.astype(BF16))

    n_tiles = N_MEM * HEADS // SUBLANES
    kc = cache_mem_k.reshape(DEPTH, n_sample, n_tiles, SUBLANES, HEAD_DIM)
    vc = cache_mem_v.reshape(DEPTH, n_sample, n_tiles, SUBLANES, HEAD_DIM)

    xp = x_prompt.reshape(batch * seq, D_MODEL)
    xs = x_sample.reshape(n_sample, D_MODEL)
    cb_p, cb_s, cc_p, cc_s, gv_s = [], [], [], [], []
    for l in range(DEPTH):
        last = l == DEPTH - 1
        xp = _ffn_call(xp, l, *ffn1, None, tm=TM_FFN, name=f"ffn1_prompt_{l}")
        xs = _ffn_call(xs, l, *ffn1, None, tm=n_sample, name=f"ffn1_sample_{l}")
        xp, tb, tc = _mix_prompt_call(xp, kmem, vmem_, l, w, batch=batch, seq=seq)
        xs, gv, nb, nc = _mix_sample_call(xs, kc, vc, state_conv_b, state_conv_c, l, w)
        xp = _ffn_call(xp, l, *ffn2, final if last else None, tm=TM_FFN, name=f"ffn2_prompt_{l}")
        xs = _ffn_call(xs, l, *ffn2, final if last else None, tm=n_sample, name=f"ffn2_sample_{l}")
        cb_p.append(tb); cc_p.append(tc); cb_s.append(nb); cc_s.append(nc); gv_s.append(gv)

    kv_shape = (DEPTH, batch, N_MEM, HEADS, HEAD_DIM)
    return (xp.reshape(batch, seq, D_MODEL), xs.reshape(n_sample, 1, D_MODEL),
            kmem.reshape(kv_shape), vmem_.reshape(kv_shape),
            jnp.stack(cb_p), jnp.stack(cb_s), jnp.stack(cc_p), jnp.stack(cc_s),
            jnp.stack(gv_s).reshape(DEPTH, n_sample, 1, HALF))
```

```python
import functools

import jax
import jax.numpy as jnp
from jax import lax
from jax.experimental import pallas as pl
from jax.experimental.pallas import tpu as pltpu

F32 = jnp.float32
BF16 = jnp.bfloat16

D_MODEL = 1024
DEPTH = 4
N_MEM = 256
CHUNK = 128
GROUPS = 4
HALF = 512
CONV_B_K = 31
CONV_C_K = 3
HEADS = 4
HEAD_DIM = 128
D_FF = 2816
EPS = 1e-6
ATTN_SCALE = HEAD_DIM ** -0.5

SUBLANES = 8
LANES = 128

FF_CHUNK = 256
N_FF_CHUNKS = D_FF // FF_CHUNK
TM_FFN = 512
TM_MIX = 512
CONV_ROWS = 32
HALO_B = 32
HALO_C = 8
SAMPLE_BLOCK = 8
VMEM_LIMIT = 56 * 1024 * 1024


def _dot(a, b):
    return jnp.dot(a, b, preferred_element_type=F32)


def _rms(x, g):
    y = x * lax.rsqrt(jnp.mean(x * x, axis=-1, keepdims=True) + EPS)
    return y * g


def _gelu(x):
    return 0.5 * x * (1.0 + lax.erf(x * (0.5 ** 0.5)))


def _layer_norm(x, g, b):
    mu = jnp.mean(x, axis=-1, keepdims=True)
    xc = x - mu
    var = jnp.mean(xc * xc, axis=-1, keepdims=True)
    return xc * lax.rsqrt(var + EPS) * g + b


def _ffn_kernel(*refs, final):
    if final:
        x_ref, g_ref, wgu_ref, wd_ref, fg_ref, o_ref = refs
    else:
        x_ref, g_ref, wgu_ref, wd_ref, o_ref = refs
    x = x_ref[...]
    h = _rms(x, g_ref[...]).astype(BF16)
    acc = None
    for c in range(N_FF_CHUNKS):
        gate = _dot(h, wgu_ref[:, c * FF_CHUNK:(c + 1) * FF_CHUNK])
        up = _dot(h, wgu_ref[:, D_FF + c * FF_CHUNK:D_FF + (c + 1) * FF_CHUNK])
        a = (jax.nn.silu(gate) * up).astype(BF16)
        d = _dot(a, wd_ref[c * FF_CHUNK:(c + 1) * FF_CHUNK, :])
        acc = d if acc is None else acc + d
    y = x + 0.5 * acc
    if final:
        y = _rms(y, fg_ref[...])
    o_ref[...] = y


def _resident(shape, layer):
    nd = len(shape)
    return pl.BlockSpec((None,) + tuple(shape), lambda *_: (layer,) + (0,) * nd,
                        pipeline_mode=pl.Buffered(1))


def _ffn_call(x, layer, norm, wgu, wd, final_norm, *, tm, name):
    m = x.shape[0]
    final = final_norm is not None
    in_specs = [
        pl.BlockSpec((tm, D_MODEL), lambda i: (i, 0)),
        _resident((1, D_MODEL), layer),
        _resident((D_MODEL, 2 * D_FF), layer),
        _resident((D_FF, D_MODEL), layer),
    ]
    args = [x, norm, wgu, wd]
    if final:
        in_specs.append(pl.BlockSpec((1, D_MODEL), lambda i: (0, 0)))
        args.append(final_norm)
    return pl.pallas_call(
        functools.partial(_ffn_kernel, final=final),
        grid=(m // tm,),
        in_specs=in_specs,
        out_specs=pl.BlockSpec((tm, D_MODEL), lambda i: (i, 0)),
        out_shape=jax.ShapeDtypeStruct((m, D_MODEL), F32),
        compiler_params=pltpu.CompilerParams(
            dimension_semantics=("arbitrary",), vmem_limit_bytes=VMEM_LIMIT),
        name=name,
    )(*args)


def _memkv_kernel(m_ref, g_ref, wk_ref, wv_ref, k_ref, v_ref):
    m = _rms(m_ref[...], g_ref[...]).astype(BF16)
    k_ref[...] = _dot(m, wk_ref[...])
    v_ref[...] = _dot(m, wv_ref[...])


def _memkv_call(mem, norm, wk, wv):
    rows = mem.shape[0]
    tm = 512
    per_layer = lambda shape: pl.BlockSpec((None,) + shape, lambda l, i: (l, 0, 0))
    out_spec = pl.BlockSpec((None, tm, HALF), lambda l, i: (l, i, 0))
    return pl.pallas_call(
        _memkv_kernel,
        grid=(DEPTH, rows // tm),
        in_specs=[pl.BlockSpec((tm, D_MODEL), lambda l, i: (i, 0)),
                  per_layer((1, D_MODEL)), per_layer((D_MODEL, HALF)), per_layer((D_MODEL, HALF))],
        out_specs=[out_spec, out_spec],
        out_shape=[jax.ShapeDtypeStruct((DEPTH, rows, HALF), F32)] * 2,
        compiler_params=pltpu.CompilerParams(
            dimension_semantics=("arbitrary", "arbitrary"), vmem_limit_bytes=VMEM_LIMIT),
        name="memkv",
    )(mem, norm, wk, wv)


def _causal_conv(ext_ref, w_ref, taps, first_row, rows):
    outs = []
    for r in range(rows // CONV_ROWS):
        base = first_row + r * CONV_ROWS
        acc = None
        for k in range(taps):
            term = w_ref[k:k + 1, :] * ext_ref[base + k:base + k + CONV_ROWS, :]
            acc = term if acc is None else acc + term
        outs.append(acc)
    return jnp.concatenate(outs, axis=0)


def _causal_conv_shifted(ext_ref, shifted_ref, w_ref, taps, first_row, rows):
    total = ext_ref.shape[0]
    ext = ext_ref[...]
    for s in range(1, SUBLANES):
        shifted_ref[s - 1] = pltpu.roll(ext, total - s, axis=0)
    outs = []
    for r in range(rows // CONV_ROWS):
        acc = None
        for k in range(taps):
            s = (first_row + k) % SUBLANES
            base = r * CONV_ROWS + first_row + k - s
            src = ext_ref if s == 0 else shifted_ref.at[s - 1]
            term = w_ref[k:k + 1, :] * src[base:base + CONV_ROWS, :]
            acc = term if acc is None else acc + term
        outs.append(acc)
    return jnp.concatenate(outs, axis=0)


def _mix_prompt_kernel(x_ref, k_ref, v_ref, norm_ref, win_ref, lng_ref, lnb_ref, ws_ref, bst_ref,
                       wa_ref, cbw_ref, cbb_ref, cblg_ref, cblb_ref, wb_ref, ccw_ref, wc_ref, wm_ref,
                       wg_ref, bg_ref, wo_ref,
                       xo_ref, tailb_ref, tailc_ref,
                       extb_ref, extc_ref, shiftb_ref):
    tm = x_ref.shape[0]
    j = pl.program_id(1)

    @pl.when(j == 0)
    def _():
        extb_ref[0:HALO_B, :] = jnp.zeros((HALO_B, HALF), F32)
        extc_ref[0:HALO_C, :] = jnp.zeros((HALO_C, HALF), F32)

    x = x_ref[...]
    h = _rms(x, norm_ref[...]).astype(BF16)

    def gate(i):
        z = _dot(h, wg_ref[:, i * D_MODEL:(i + 1) * D_MODEL]) + bg_ref[:, i * D_MODEL:(i + 1) * D_MODEL]
        return jax.nn.sigmoid(z)

    za = _gelu(_dot(h, win_ref[:, 0:2 * HALF]))
    u = za[:, :HALF]
    v = _layer_norm(za[:, HALF:], lng_ref[...], lnb_ref[...]).astype(BF16)
    n_chunks = tm // CHUNK
    row = lax.broadcasted_iota(jnp.int32, (CHUNK, CHUNK), 0)
    col = lax.broadcasted_iota(jnp.int32, (CHUNK, CHUNK), 1)
    pieces = [[None] * GROUPS for _ in range(n_chunks)]
    for g in range(GROUPS):
        ws = jnp.where(row >= col, ws_ref[g], 0.0).astype(BF16)
        rhs = jnp.concatenate(
            [v[c * CHUNK:(c + 1) * CHUNK, g * LANES:(g + 1) * LANES] for c in range(n_chunks)], axis=1)
        res = _dot(ws, rhs) + bst_ref[:, g:g + 1]
        for c in range(n_chunks):
            pieces[c][g] = res[:, c * CHUNK:(c + 1) * CHUNK]
    mixed = jnp.concatenate([jnp.concatenate(p, axis=1) for p in pieces], axis=0)
    merged = gate(0) * _dot((u * mixed).astype(BF16), wa_ref[...])

    zb = _dot(h, win_ref[:, 2 * HALF:4 * HALF])
    extb_ref[HALO_B:HALO_B + tm, :] = zb[:, :HALF] * jax.nn.sigmoid(zb[:, HALF:])
    yb = _causal_conv_shifted(extb_ref, shiftb_ref, cbw_ref, CONV_B_K, HALO_B - (CONV_B_K - 1), tm)
    yb = yb + cbb_ref[...]
    yb = jax.nn.silu(_layer_norm(yb, cblg_ref[...], cblb_ref[...]))
    merged = merged + gate(1) * _dot(yb.astype(BF16), wb_ref[...])

    zc = _dot(h, win_ref[:, 4 * HALF:7 * HALF])
    extc_ref[HALO_C:HALO_C + tm, :] = zc[:, HALF:2 * HALF] * zc[:, 2 * HALF:]
    yc = _causal_conv(extc_ref, ccw_ref, CONV_C_K, HALO_C - (CONV_C_K - 1), tm)
    merged = merged + gate(2) * _dot((zc[:, :HALF] * yc).astype(BF16), wc_ref[...])

    q = _dot(h, win_ref[:, 7 * HALF:8 * HALF])
    kb = k_ref[...].astype(BF16)
    vb = v_ref[...].astype(BF16)
    ones = jnp.ones((N_MEM, HEAD_DIM), BF16)
    heads = []
    for hd in range(HEADS):
        sl = slice(hd * HEAD_DIM, (hd + 1) * HEAD_DIM)
        s = lax.dot_general(q[:, sl].astype(BF16), kb[:, sl], (((1,), (1,)), ((), ())),
                            preferred_element_type=F32) * ATTN_SCALE
        e = jnp.exp(s - jnp.max(s, axis=-1, keepdims=True)).astype(BF16)
        r = _dot(e, jnp.concatenate([vb[:, sl], ones], axis=1))
        heads.append(r[:, :HEAD_DIM] / r[:, HEAD_DIM:])
    om = jnp.concatenate(heads, axis=1).astype(BF16)
    merged = merged + gate(3) * _dot(om, wm_ref[...])

    xo_ref[...] = x + _dot(merged.astype(BF16), wo_ref[...])

    @pl.when(j == pl.num_programs(1) - 1)
    def _():
        tailb_ref[...] = extb_ref[tm + HALO_B - (CONV_B_K - 1):tm + HALO_B, :]
        tailc_ref[...] = extc_ref[tm + HALO_C - (CONV_C_K - 1):tm + HALO_C, :]

    extb_ref[0:HALO_B, :] = extb_ref[tm:tm + HALO_B, :]
    extc_ref[0:HALO_C, :] = extc_ref[tm:tm + HALO_C, :]


def _mix_prompt_call(x, kmem, vmem_, layer, w, *, batch, seq):
    tm = TM_MIX
    nj = seq // tm
    in_specs = [
        pl.BlockSpec((tm, D_MODEL), lambda b, j: (b * nj + j, 0)),
        pl.BlockSpec((None, N_MEM, HALF), lambda b, j: (layer, b, 0)),
        pl.BlockSpec((None, N_MEM, HALF), lambda b, j: (layer, b, 0)),
        _resident((1, D_MODEL), layer),
        _resident((D_MODEL, 8 * HALF), layer),
        _resident((1, HALF), layer),
        _resident((1, HALF), layer),
        _resident((GROUPS, CHUNK, CHUNK), layer),
        _resident((CHUNK, GROUPS), layer),
        _resident((HALF, D_MODEL), layer),
        _resident((CONV_B_K, HALF), layer),
        _resident((1, HALF), layer),
        _resident((1, HALF), layer),
        _resident((1, HALF), layer),
        _resident((HALF, D_MODEL), layer),
        _resident((CONV_C_K, HALF), layer),
        _resident((HALF, D_MODEL), layer),
        _resident((HALF, D_MODEL), layer),
        _resident((D_MODEL, 4 * D_MODEL), layer),
        _resident((1, 4 * D_MODEL), layer),
        _resident((D_MODEL, D_MODEL), layer),
    ]
    out_specs = [
        pl.BlockSpec((tm, D_MODEL), lambda b, j: (b * nj + j, 0)),
        pl.BlockSpec((None, CONV_B_K - 1, HALF), lambda b, j: (b, 0, 0)),
        pl.BlockSpec((None, CONV_C_K - 1, HALF), lambda b, j: (b, 0, 0)),
    ]
    out_shape = [
        jax.ShapeDtypeStruct((batch * seq, D_MODEL), F32),
        jax.ShapeDtypeStruct((batch, CONV_B_K - 1, HALF), F32),
        jax.ShapeDtypeStruct((batch, CONV_C_K - 1, HALF), F32),
    ]
    return pl.pallas_call(
        _mix_prompt_kernel,
        grid=(batch, nj),
        in_specs=in_specs,
        out_specs=out_specs,
        out_shape=out_shape,
        scratch_shapes=[pltpu.VMEM((HALO_B + tm, HALF), F32), pltpu.VMEM((HALO_C + tm, HALF), F32),
                        pltpu.VMEM((SUBLANES - 1, HALO_B + tm, HALF), F32)],
        compiler_params=pltpu.CompilerParams(
            dimension_semantics=("arbitrary", "arbitrary"), vmem_limit_bytes=VMEM_LIMIT),
        name=f"mix_prompt_{layer}",
    )(x, kmem, vmem_, w["mix_norm"], w["w_in"], w["gmlp_ln_g"], w["gmlp_ln_b"], w["gmlp_w_s"],
      w["gmlp_b_s_t"], w["gmlp_w_out"], w["conv_b_w"], w["conv_b_bias"], w["conv_b_ln_g"],
      w["conv_b_ln_b"], w["conv_b_w_out"], w["conv_c_w"], w["conv_c_w_out"], w["mem_w_out"],
      w["w_branch_gate"], w["b_branch_gate"], w["w_o"])


def _mix_sample_kernel(x_ref, k_ref, v_ref, stb_ref, stc_ref, norm_ref, win_ref, lng_ref, lnb_ref,
                       ws0_ref, bs0_ref, wa_ref, cbw_ref, cbb_ref, cblg_ref, cblb_ref, wb_ref, ccw_ref,
                       wc_ref, wm_ref, wg_ref, bg_ref, wo_ref,
                       xo_ref, gv_ref, nstb_ref, nstc_ref,
                       h_scr, pa_scr, xinb_scr, gatec_scr, xinc_scr, q_scr, yb_scr, yc_scr, o_scr):
    i = pl.program_id(0)
    sb = stb_ref.shape[1]
    r0 = pl.multiple_of(i * sb, sb)

    @pl.when(i == 0)
    def _():
        h = _rms(x_ref[...], norm_ref[...]).astype(BF16)
        h_scr[...] = h
        za = _gelu(_dot(h, win_ref[:, 0:2 * HALF]))
        v = _layer_norm(za[:, HALF:], lng_ref[...], lnb_ref[...])
        gv_ref[...] = v
        pa_scr[...] = za[:, :HALF] * (ws0_ref[...] * v + bs0_ref[...])
        zb = _dot(h, win_ref[:, 2 * HALF:4 * HALF])
        xinb_scr[...] = zb[:, :HALF] * jax.nn.sigmoid(zb[:, HALF:])
        zc = _dot(h, win_ref[:, 4 * HALF:7 * HALF])
        gatec_scr[...] = zc[:, :HALF]
        xinc_scr[...] = zc[:, HALF:2 * HALF] * zc[:, 2 * HALF:]
        q = _dot(h, win_ref[:, 7 * HALF:8 * HALF])
        for hd in range(HEADS):
            qh = q[:, hd * HEAD_DIM:(hd + 1) * HEAD_DIM]
            q_scr[:, hd, :] = qh
            q_scr[:, hd + HEADS, :] = qh

    xin_b = xinb_scr[pl.ds(r0, sb), :]
    acc = cbw_ref[CONV_B_K - 1:CONV_B_K, :] * xin_b
    for k in range(CONV_B_K - 1):
        acc = acc + cbw_ref[k:k + 1, :] * stb_ref[k]
    yb_scr[pl.ds(r0, sb), :] = acc
    nstb_ref[0:CONV_B_K - 2] = stb_ref[1:CONV_B_K - 1]
    nstb_ref[CONV_B_K - 2] = xin_b

    xin_c = xinc_scr[pl.ds(r0, sb), :]
    yc_scr[pl.ds(r0, sb), :] = (ccw_ref[0:1, :] * stc_ref[0] + ccw_ref[1:2, :] * stc_ref[1]
                                + ccw_ref[2:3, :] * xin_c)
    nstc_ref[0] = stc_ref[1]
    nstc_ref[1] = xin_c

    ones = jnp.ones((HEAD_DIM, LANES), BF16)
    n_tiles = N_MEM * HEADS // SUBLANES
    for s in range(sb):
        q8 = q_scr[r0 + s]
        prod = (k_ref[s] * q8[None]).reshape(n_tiles * SUBLANES, HEAD_DIM)
        sc = (_dot(prod.astype(BF16), ones) * ATTN_SCALE).reshape(n_tiles, SUBLANES, LANES)
        mx = jnp.max(sc, axis=0)
        mx = jnp.maximum(mx, pltpu.roll(mx, HEADS, axis=0))
        e = jnp.exp(sc - mx[None])
        num = jnp.sum(e * v_ref[s], axis=0)
        den = jnp.sum(e, axis=0)
        num = num + pltpu.roll(num, HEADS, axis=0)
        den = den + pltpu.roll(den, HEADS, axis=0)
        o_scr[r0 + s] = num / den

    @pl.when(i == pl.num_programs(0) - 1)
    def _():
        h = h_scr[...]

        def gate(n):
            z = _dot(h, wg_ref[:, n * D_MODEL:(n + 1) * D_MODEL]) + bg_ref[:, n * D_MODEL:(n + 1) * D_MODEL]
            return jax.nn.sigmoid(z)

        merged = gate(0) * _dot(pa_scr[...].astype(BF16), wa_ref[...])
        yb = jax.nn.silu(_layer_norm(yb_scr[...] + cbb_ref[...], cblg_ref[...], cblb_ref[...]))
        merged = merged + gate(1) * _dot(yb.astype(BF16), wb_ref[...])
        merged = merged + gate(2) * _dot((gatec_scr[...] * yc_scr[...]).astype(BF16), wc_ref[...])
        om = None
        for hd in range(HEADS):
            part = _dot(o_scr[:, hd, :].astype(BF16), wm_ref[hd * HEAD_DIM:(hd + 1) * HEAD_DIM, :])
            om = part if om is None else om + part
        merged = merged + gate(3) * om
        xo_ref[...] = x_ref[...] + _dot(merged.astype(BF16), wo_ref[...])


def _mix_sample_call(x, kc, vc, stb, stc, layer, w):
    n = x.shape[0]
    sb = SAMPLE_BLOCK
    n_tiles = N_MEM * HEADS // SUBLANES
    const = lambda shape: pl.BlockSpec(shape, lambda i: (0,) * len(shape))
    in_specs = [
        const((n, D_MODEL)),
        pl.BlockSpec((None, sb, n_tiles, SUBLANES, HEAD_DIM), lambda i: (layer, i, 0, 0, 0)),
        pl.BlockSpec((None, sb, n_tiles, SUBLANES, HEAD_DIM), lambda i: (layer, i, 0, 0, 0)),
        pl.BlockSpec((None, CONV_B_K - 1, sb, HALF), lambda i: (layer, 0, i, 0)),
        pl.BlockSpec((None, CONV_C_K - 1, sb, HALF), lambda i: (layer, 0, i, 0)),
        _resident((1, D_MODEL), layer),
        _resident((D_MODEL, 8 * HALF), layer),
        _resident((1, HALF), layer),
        _resident((1, HALF), layer),
        _resident((1, HALF), layer),
        _resident((1, HALF), layer),
        _resident((HALF, D_MODEL), layer),
        _resident((CONV_B_K, HALF), layer),
        _resident((1, HALF), layer),
        _resident((1, HALF), layer),
        _resident((1, HALF), layer),
        _resident((HALF, D_MODEL), layer),
        _resident((CONV_C_K, HALF), layer),
        _resident((HALF, D_MODEL), layer),
        _resident((HALF, D_MODEL), layer),
        _resident((D_MODEL, 4 * D_MODEL), layer),
        _resident((1, 4 * D_MODEL), layer),
        _resident((D_MODEL, D_MODEL), layer),
    ]
    out_specs = [
        const((n, D_MODEL)),
        const((n, HALF)),
        pl.BlockSpec((CONV_B_K - 1, sb, HALF), lambda i: (0, i, 0)),
        pl.BlockSpec((CONV_C_K - 1, sb, HALF), lambda i: (0, i, 0)),
    ]
    out_shape = [
        jax.ShapeDtypeStruct((n, D_MODEL), F32),
        jax.ShapeDtypeStruct((n, HALF), F32),
        jax.ShapeDtypeStruct((CONV_B_K - 1, n, HALF), F32),
        jax.ShapeDtypeStruct((CONV_C_K - 1, n, HALF), F32),
    ]
    scratch = [
        pltpu.VMEM((n, D_MODEL), BF16),
        pltpu.VMEM((n, HALF), F32),
        pltpu.VMEM((n, HALF), F32),
        pltpu.VMEM((n, HALF), F32),
        pltpu.VMEM((n, HALF), F32),
        pltpu.VMEM((n, SUBLANES, HEAD_DIM), F32),
        pltpu.VMEM((n, HALF), F32),
        pltpu.VMEM((n, HALF), F32),
        pltpu.VMEM((n, SUBLANES, HEAD_DIM), F32),
    ]
    return pl.pallas_call(
        _mix_sample_kernel,
        grid=(n // sb,),
        in_specs=in_specs,
        out_specs=out_specs,
        out_shape=out_shape,
        scratch_shapes=scratch,
        compiler_params=pltpu.CompilerParams(
            dimension_semantics=("arbitrary",), vmem_limit_bytes=VMEM_LIMIT),
        name=f"mix_sample_{layer}",
    )(x, kc, vc, stb, stc, w["mix_norm"], w["w_in"], w["gmlp_ln_g"], w["gmlp_ln_b"], w["gmlp_ws0"],
      w["gmlp_bs0"], w["gmlp_w_out"], w["conv_b_w"], w["conv_b_bias"], w["conv_b_ln_g"],
      w["conv_b_ln_b"], w["conv_b_w_out"], w["conv_c_w"], w["conv_c_w_out"], w["mem_w_out"],
      w["w_branch_gate"], w["b_branch_gate"], w["w_o"])


def kernel(x_prompt, x_sample, mem_prompt, state_conv_b, state_conv_c, cache_mem_k, cache_mem_v,
           ffn1_norm, ffn1_w_gate_up, ffn1_w_down, mix_norm, w_in,
           gmlp_ln_g, gmlp_ln_b, gmlp_w_s, gmlp_b_s, gmlp_w_out,
           conv_b_w, conv_b_bias, conv_b_ln_g, conv_b_ln_b, conv_b_w_out,
           conv_c_w, conv_c_w_out, mem_norm, mem_w_k, mem_w_v, mem_w_out,
           w_branch_gate, b_branch_gate, w_o, ffn2_norm, ffn2_w_gate_up, ffn2_w_down, final_norm):
    batch, seq, _ = x_prompt.shape
    n_sample = x_sample.shape[0]
    row = lambda a: a.reshape(DEPTH, 1, a.shape[-1])

    w = {
        "mix_norm": row(mix_norm), "w_in": w_in.astype(BF16),
        "gmlp_ln_g": row(gmlp_ln_g), "gmlp_ln_b": row(gmlp_ln_b),
        "gmlp_w_s": gmlp_w_s, "gmlp_b_s_t": jnp.swapaxes(gmlp_b_s, 1, 2),
        "gmlp_ws0": row(jnp.repeat(gmlp_w_s[:, :, 0, 0], LANES, axis=1)),
        "gmlp_bs0": row(jnp.repeat(gmlp_b_s[:, :, 0], LANES, axis=1)),
        "gmlp_w_out": gmlp_w_out.astype(BF16),
        "conv_b_w": conv_b_w, "conv_b_bias": row(conv_b_bias),
        "conv_b_ln_g": row(conv_b_ln_g), "conv_b_ln_b": row(conv_b_ln_b),
        "conv_b_w_out": conv_b_w_out.astype(BF16),
        "conv_c_w": conv_c_w, "conv_c_w_out": conv_c_w_out.astype(BF16),
        "mem_w_out": mem_w_out.astype(BF16),
        "w_branch_gate": w_branch_gate.astype(BF16), "b_branch_gate": row(b_branch_gate),
        "w_o": w_o.astype(BF16),
    }
    ffn1 = (row(ffn1_norm), ffn1_w_gate_up.astype(BF16), ffn1_w_down.astype(BF16))
    ffn2 = (row(ffn2_norm), ffn2_w_gate_up.astype(BF16), ffn2_w_down.astype(BF16))
    final = final_norm.reshape(1, D_MODEL)

    kmem, vmem_ = _memkv_call(mem_prompt.reshape(batch * N_MEM, D_MODEL), row(mem_norm),
                              mem_w_k.astype(BF16), mem_w_v.astype(BF16))

    n_tiles = N_MEM * HEADS // SUBLANES
    kc = cache_mem_k.reshape(DEPTH, n_sample, n_tiles, SUBLANES, HEAD_DIM)
    vc = cache_mem_v.reshape(DEPTH, n_sample, n_tiles, SUBLANES, HEAD_DIM)

    stb = jnp.swapaxes(state_conv_b, 1, 2)
    stc = jnp.swapaxes(state_conv_c, 1, 2)

    xp = x_prompt.reshape(batch * seq, D_MODEL)
    xs = x_sample.reshape(n_sample, D_MODEL)
    cb_p, cb_s, cc_p, cc_s, gv_s = [], [], [], [], []
    for l in range(DEPTH):
        last = l == DEPTH - 1
        xp = _ffn_call(xp, l, *ffn1, None, tm=TM_FFN, name=f"ffn1_prompt_{l}")
        xs = _ffn_call(xs, l, *ffn1, None, tm=n_sample, name=f"ffn1_sample_{l}")
        xp, tb, tc = _mix_prompt_call(xp, kmem, vmem_, l, w, batch=batch, seq=seq)
        xs, gv, nb, nc = _mix_sample_call(xs, kc, vc, stb, stc, l, w)
        xp = _ffn_call(xp, l, *ffn2, final if last else None, tm=TM_FFN, name=f"ffn2_prompt_{l}")
        xs = _ffn_call(xs, l, *ffn2, final if last else None, tm=n_sample, name=f"ffn2_sample_{l}")
        cb_p.append(tb); cc_p.append(tc); cb_s.append(nb); cc_s.append(nc); gv_s.append(gv)

    kv_shape = (DEPTH, batch, N_MEM, HEADS, HEAD_DIM)
    return (xp.reshape(batch, seq, D_MODEL), xs.reshape(n_sample, 1, D_MODEL),
            kmem.reshape(kv_shape), vmem_.reshape(kv_shape),
            jnp.stack(cb_p), jnp.swapaxes(jnp.stack(cb_s), 1, 2),
            jnp.stack(cc_p), jnp.swapaxes(jnp.stack(cc_s), 1, 2),
            jnp.stack(gv_s).reshape(DEPTH, n_sample, 1, HALF))
```

```python
import functools

import jax
import jax.numpy as jnp
from jax import lax
from jax.experimental import pallas as pl
from jax.experimental.pallas import tpu as pltpu

F32 = jnp.float32
BF16 = jnp.bfloat16

D_MODEL = 1024
DEPTH = 4
N_MEM = 256
CHUNK = 128
GROUPS = 4
HALF = 512
CONV_B_K = 31
CONV_C_K = 3
HEADS = 4
HEAD_DIM = 128
D_FF = 2816
EPS = 1e-6
ATTN_SCALE = HEAD_DIM ** -0.5

SUBLANES = 8
LANES = 128

FF_CHUNK = 256
N_FF_CHUNKS = D_FF // FF_CHUNK
TM_FFN = 512
TM_MIX = 512
CONV_ROWS = 32
HALO_B = 32
HALO_C = 8
SAMPLE_BLOCK = 8
VMEM_LIMIT = 56 * 1024 * 1024


def _dot(a, b):
    return jnp.dot(a, b, preferred_element_type=F32)


def _rms(x, g):
    y = x * lax.rsqrt(jnp.mean(x * x, axis=-1, keepdims=True) + EPS)
    return y * g


def _gelu(x):
    return 0.5 * x * (1.0 + lax.erf(x * (0.5 ** 0.5)))


def _layer_norm(x, g, b):
    mu = jnp.mean(x, axis=-1, keepdims=True)
    xc = x - mu
    var = jnp.mean(xc * xc, axis=-1, keepdims=True)
    return xc * lax.rsqrt(var + EPS) * g + b


def _ffn_kernel(*refs, final):
    if final:
        xp_ref, xs_ref, g_ref, wgu_ref, wd_ref, fg_ref, op_ref, os_ref = refs
    else:
        xp_ref, xs_ref, g_ref, wgu_ref, wd_ref, op_ref, os_ref = refs

    def rows(x_ref, o_ref):
        x = x_ref[...]
        h = _rms(x, g_ref[...]).astype(BF16)
        acc = None
        for c in range(N_FF_CHUNKS):
            gate = _dot(h, wgu_ref[:, c * FF_CHUNK:(c + 1) * FF_CHUNK])
            up = _dot(h, wgu_ref[:, D_FF + c * FF_CHUNK:D_FF + (c + 1) * FF_CHUNK])
            a = (jax.nn.silu(gate) * up).astype(BF16)
            d = _dot(a, wd_ref[c * FF_CHUNK:(c + 1) * FF_CHUNK, :])
            acc = d if acc is None else acc + d
        y = x + 0.5 * acc
        if final:
            y = _rms(y, fg_ref[...])
        o_ref[...] = y

    rows(xp_ref, op_ref)

    @pl.when(pl.program_id(0) == pl.num_programs(0) - 1)
    def _():
        rows(xs_ref, os_ref)


def _resident(shape, layer):
    nd = len(shape)
    return pl.BlockSpec((None,) + tuple(shape), lambda *_: (layer,) + (0,) * nd,
                        pipeline_mode=pl.Buffered(1))


def _ffn_call(xp, xs, layer, norm, wgu, wd, final_norm, *, name):
    m, n = xp.shape[0], xs.shape[0]
    tm = TM_FFN
    final = final_norm is not None
    sample_spec = pl.BlockSpec((n, D_MODEL), lambda i: (0, 0))
    in_specs = [
        pl.BlockSpec((tm, D_MODEL), lambda i: (i, 0)),
        sample_spec,
        _resident((1, D_MODEL), layer),
        _resident((D_MODEL, 2 * D_FF), layer),
        _resident((D_FF, D_MODEL), layer),
    ]
    args = [xp, xs, norm, wgu, wd]
    if final:
        in_specs.append(pl.BlockSpec((1, D_MODEL), lambda i: (0, 0)))
        args.append(final_norm)
    return pl.pallas_call(
        functools.partial(_ffn_kernel, final=final),
        grid=(m // tm,),
        in_specs=in_specs,
        out_specs=[pl.BlockSpec((tm, D_MODEL), lambda i: (i, 0)), sample_spec],
        out_shape=[jax.ShapeDtypeStruct((m, D_MODEL), F32), jax.ShapeDtypeStruct((n, D_MODEL), F32)],
        compiler_params=pltpu.CompilerParams(
            dimension_semantics=("arbitrary",), vmem_limit_bytes=VMEM_LIMIT),
        name=name,
    )(*args)


def _memkv_kernel(m_ref, g_ref, wk_ref, wv_ref, k_ref, v_ref):
    m = _rms(m_ref[...], g_ref[...]).astype(BF16)
    k_ref[...] = _dot(m, wk_ref[...])
    v_ref[...] = _dot(m, wv_ref[...])


def _memkv_call(mem, norm, wk, wv):
    rows = mem.shape[0]
    tm = 512
    per_layer = lambda shape: pl.BlockSpec((None,) + shape, lambda l, i: (l, 0, 0))
    out_spec = pl.BlockSpec((None, tm, HALF), lambda l, i: (l, i, 0))
    return pl.pallas_call(
        _memkv_kernel,
        grid=(DEPTH, rows // tm),
        in_specs=[pl.BlockSpec((tm, D_MODEL), lambda l, i: (i, 0)),
                  per_layer((1, D_MODEL)), per_layer((D_MODEL, HALF)), per_layer((D_MODEL, HALF))],
        out_specs=[out_spec, out_spec],
        out_shape=[jax.ShapeDtypeStruct((DEPTH, rows, HALF), F32)] * 2,
        compiler_params=pltpu.CompilerParams(
            dimension_semantics=("arbitrary", "arbitrary"), vmem_limit_bytes=VMEM_LIMIT),
        name="memkv",
    )(mem, norm, wk, wv)


def _causal_conv(ext_ref, w_ref, taps, first_row, rows):
    outs = []
    for r in range(rows // CONV_ROWS):
        base = first_row + r * CONV_ROWS
        acc = None
        for k in range(taps):
            term = w_ref[k:k + 1, :] * ext_ref[base + k:base + k + CONV_ROWS, :]
            acc = term if acc is None else acc + term
        outs.append(acc)
    return jnp.concatenate(outs, axis=0)


def _shifted_copies(ext_ref, shifted_ref):
    total = ext_ref.shape[0]
    tiles = ext_ref[...].reshape(total // SUBLANES, SUBLANES, HALF)
    sub = lax.broadcasted_iota(jnp.int32, (1, SUBLANES, HALF), 1)
    for s in range(1, SUBLANES):
        rot = pltpu.roll(tiles, SUBLANES - s, axis=1)
        nxt = jnp.concatenate([rot[1:], rot[:1]], axis=0)
        shifted_ref[s - 1] = jnp.where(sub < SUBLANES - s, rot, nxt).reshape(total, HALF)


def _conv_chunks(ext_ref, shifted_ref, w_ref, out_ref, taps, first_row, chunks):
    for r in chunks:
        acc = None
        for k in range(taps):
            s = (first_row + k) % SUBLANES
            base = r * CONV_ROWS + first_row + k - s
            src = ext_ref if s == 0 else shifted_ref.at[s - 1]
            win = src[base:base + CONV_ROWS, :].reshape(CONV_ROWS // SUBLANES, SUBLANES, HALF)
            term = w_ref[k][None] * win
            acc = term if acc is None else acc + term
        out_ref[r * CONV_ROWS:(r + 1) * CONV_ROWS, :] = acc.reshape(CONV_ROWS, HALF)


def _mix_prompt_kernel(x_ref, k_ref, v_ref, norm_ref, win_ref, lng_ref, lnb_ref, ws_ref, bst_ref,
                       wa_ref, cbw_ref, cbb_ref, cblg_ref, cblb_ref, wb_ref, ccw_ref, wc_ref, wm_ref,
                       wg_ref, bg_ref, wo_ref,
                       xo_ref, tailb_ref, tailc_ref,
                       extb_ref, extc_ref, shiftb_ref, h_scr, yb_scr, merged_scr):
    tm = x_ref.shape[0]
    j = pl.program_id(1)
    n_conv = tm // CONV_ROWS
    conv_split = (0, 6 * n_conv // 16, 11 * n_conv // 16, n_conv)

    def gate(i):
        z = _dot(h_scr[...], wg_ref[:, i * D_MODEL:(i + 1) * D_MODEL])
        return jax.nn.sigmoid(z + bg_ref[:, i * D_MODEL:(i + 1) * D_MODEL])

    def conv_b(part):
        _conv_chunks(extb_ref, shiftb_ref, cbw_ref, yb_scr, CONV_B_K, HALO_B - (CONV_B_K - 1),
                     range(conv_split[part], conv_split[part + 1]))

    @pl.when(j == 0)
    def _():
        extb_ref[0:HALO_B, :] = jnp.zeros((HALO_B, HALF), F32)
        extc_ref[0:HALO_C, :] = jnp.zeros((HALO_C, HALF), F32)

    h = _rms(x_ref[...], norm_ref[...]).astype(BF16)
    h_scr[...] = h

    zb = _dot(h, win_ref[:, 2 * HALF:4 * HALF])
    extb_ref[HALO_B:HALO_B + tm, :] = zb[:, :HALF] * jax.nn.sigmoid(zb[:, HALF:])
    _shifted_copies(extb_ref, shiftb_ref)

    conv_b(0)
    za = _gelu(_dot(h_scr[...], win_ref[:, 0:2 * HALF]))
    u = za[:, :HALF]
    v = _layer_norm(za[:, HALF:], lng_ref[...], lnb_ref[...]).astype(BF16)
    n_chunks = tm // CHUNK
    row = lax.broadcasted_iota(jnp.int32, (CHUNK, CHUNK), 0)
    col = lax.broadcasted_iota(jnp.int32, (CHUNK, CHUNK), 1)
    pieces = [[None] * GROUPS for _ in range(n_chunks)]
    for g in range(GROUPS):
        ws = jnp.where(row >= col, ws_ref[g], 0.0).astype(BF16)
        rhs = jnp.concatenate(
            [v[c * CHUNK:(c + 1) * CHUNK, g * LANES:(g + 1) * LANES] for c in range(n_chunks)], axis=1)
        res = _dot(ws, rhs) + bst_ref[:, g:g + 1]
        for c in range(n_chunks):
            pieces[c][g] = res[:, c * CHUNK:(c + 1) * CHUNK]
    mixed = jnp.concatenate([jnp.concatenate(p, axis=1) for p in pieces], axis=0)
    merged_scr[...] = gate(0) * _dot((u * mixed).astype(BF16), wa_ref[...])

    conv_b(1)
    zc = _dot(h_scr[...], win_ref[:, 4 * HALF:7 * HALF])
    extc_ref[HALO_C:HALO_C + tm, :] = zc[:, HALF:2 * HALF] * zc[:, 2 * HALF:]
    yc = _causal_conv(extc_ref, ccw_ref, CONV_C_K, HALO_C - (CONV_C_K - 1), tm)
    merged_scr[...] += gate(2) * _dot((zc[:, :HALF] * yc).astype(BF16), wc_ref[...])

    conv_b(2)
    q = _dot(h_scr[...], win_ref[:, 7 * HALF:8 * HALF])
    kb = k_ref[...].astype(BF16)
    vb = v_ref[...].astype(BF16)
    ones = jnp.ones((N_MEM, HEAD_DIM), BF16)
    heads = []
    for hd in range(HEADS):
        sl = slice(hd * HEAD_DIM, (hd + 1) * HEAD_DIM)
        s = lax.dot_general(q[:, sl].astype(BF16), kb[:, sl], (((1,), (1,)), ((), ())),
                            preferred_element_type=F32) * ATTN_SCALE
        e = jnp.exp(s - jnp.max(s, axis=-1, keepdims=True)).astype(BF16)
        r = _dot(e, jnp.concatenate([vb[:, sl], ones], axis=1))
        heads.append(r[:, :HEAD_DIM] / r[:, HEAD_DIM:])
    om = jnp.concatenate(heads, axis=1).astype(BF16)
    merged_scr[...] += gate(3) * _dot(om, wm_ref[...])

    yb = yb_scr[...] + cbb_ref[...]
    yb = jax.nn.silu(_layer_norm(yb, cblg_ref[...], cblb_ref[...]))
    merged = merged_scr[...] + gate(1) * _dot(yb.astype(BF16), wb_ref[...])
    xo_ref[...] = x_ref[...] + _dot(merged.astype(BF16), wo_ref[...])

    @pl.when(j == pl.num_programs(1) - 1)
    def _():
        tailb_ref[...] = extb_ref[tm + HALO_B - (CONV_B_K - 1):tm + HALO_B, :]
        tailc_ref[...] = extc_ref[tm + HALO_C - (CONV_C_K - 1):tm + HALO_C, :]

    extb_ref[0:HALO_B, :] = extb_ref[tm:tm + HALO_B, :]
    extc_ref[0:HALO_C, :] = extc_ref[tm:tm + HALO_C, :]


def _mix_prompt_call(x, kmem, vmem_, layer, w, *, batch, seq):
    tm = TM_MIX
    nj = seq // tm
    in_specs = [
        pl.BlockSpec((tm, D_MODEL), lambda b, j: (b * nj + j, 0)),
        pl.BlockSpec((None, N_MEM, HALF), lambda b, j: (layer, b, 0)),
        pl.BlockSpec((None, N_MEM, HALF), lambda b, j: (layer, b, 0)),
        _resident((1, D_MODEL), layer),
        _resident((D_MODEL, 8 * HALF), layer),
        _resident((1, HALF), layer),
        _resident((1, HALF), layer),
        _resident((GROUPS, CHUNK, CHUNK), layer),
        _resident((CHUNK, GROUPS), layer),
        _resident((HALF, D_MODEL), layer),
        _resident((CONV_B_K, SUBLANES, HALF), layer),
        _resident((1, HALF), layer),
        _resident((1, HALF), layer),
        _resident((1, HALF), layer),
        _resident((HALF, D_MODEL), layer),
        _resident((CONV_C_K, HALF), layer),
        _resident((HALF, D_MODEL), layer),
        _resident((HALF, D_MODEL), layer),
        _resident((D_MODEL, 4 * D_MODEL), layer),
        _resident((1, 4 * D_MODEL), layer),
        _resident((D_MODEL, D_MODEL), layer),
    ]
    out_specs = [
        pl.BlockSpec((tm, D_MODEL), lambda b, j: (b * nj + j, 0)),
        pl.BlockSpec((None, CONV_B_K - 1, HALF), lambda b, j: (b, 0, 0)),
        pl.BlockSpec((None, CONV_C_K - 1, HALF), lambda b, j: (b, 0, 0)),
    ]
    out_shape = [
        jax.ShapeDtypeStruct((batch * seq, D_MODEL), F32),
        jax.ShapeDtypeStruct((batch, CONV_B_K - 1, HALF), F32),
        jax.ShapeDtypeStruct((batch, CONV_C_K - 1, HALF), F32),
    ]
    return pl.pallas_call(
        _mix_prompt_kernel,
        grid=(batch, nj),
        in_specs=in_specs,
        out_specs=out_specs,
        out_shape=out_shape,
        scratch_shapes=[pltpu.VMEM((HALO_B + tm, HALF), F32), pltpu.VMEM((HALO_C + tm, HALF), F32),
                        pltpu.VMEM((SUBLANES - 1, HALO_B + tm, HALF), F32),
                        pltpu.VMEM((tm, D_MODEL), BF16),
                        pltpu.VMEM((tm, HALF), F32),
                        pltpu.VMEM((tm, D_MODEL), F32)],
        compiler_params=pltpu.CompilerParams(
            dimension_semantics=("arbitrary", "arbitrary"), vmem_limit_bytes=VMEM_LIMIT),
        name=f"mix_prompt_{layer}",
    )(x, kmem, vmem_, w["mix_norm"], w["w_in"], w["gmlp_ln_g"], w["gmlp_ln_b"], w["gmlp_w_s"],
      w["gmlp_b_s_t"], w["gmlp_w_out"], w["conv_b_w8"], w["conv_b_bias"], w["conv_b_ln_g"],
      w["conv_b_ln_b"], w["conv_b_w_out"], w["conv_c_w"], w["conv_c_w_out"], w["mem_w_out"],
      w["w_branch_gate"], w["b_branch_gate"], w["w_o"])


def _mix_sample_kernel(x_ref, k_ref, v_ref, stb_ref, stc_ref, norm_ref, win_ref, lng_ref, lnb_ref,
                       ws0_ref, bs0_ref, wa_ref, cbw_ref, cbb_ref, cblg_ref, cblb_ref, wb_ref, ccw_ref,
                       wc_ref, wm_ref, wg_ref, bg_ref, wo_ref,
                       xo_ref, gv_ref, nstb_ref, nstc_ref,
                       h_scr, pa_scr, xinb_scr, gatec_scr, xinc_scr, q_scr, yb_scr, yc_scr, o_scr):
    i = pl.program_id(0)
    sb = stb_ref.shape[1]
    r0 = pl.multiple_of(i * sb, sb)

    @pl.when(i == 0)
    def _():
        h = _rms(x_ref[...], norm_ref[...]).astype(BF16)
        h_scr[...] = h
        za = _gelu(_dot(h, win_ref[:, 0:2 * HALF]))
        v = _layer_norm(za[:, HALF:], lng_ref[...], lnb_ref[...])
        gv_ref[...] = v
        pa_scr[...] = za[:, :HALF] * (ws0_ref[...] * v + bs0_ref[...])
        zb = _dot(h, win_ref[:, 2 * HALF:4 * HALF])
        xinb_scr[...] = zb[:, :HALF] * jax.nn.sigmoid(zb[:, HALF:])
        zc = _dot(h, win_ref[:, 4 * HALF:7 * HALF])
        gatec_scr[...] = zc[:, :HALF]
        xinc_scr[...] = zc[:, HALF:2 * HALF] * zc[:, 2 * HALF:]
        q = _dot(h, win_ref[:, 7 * HALF:8 * HALF])
        for hd in range(HEADS):
            qh = q[:, hd * HEAD_DIM:(hd + 1) * HEAD_DIM]
            q_scr[:, hd, :] = qh
            q_scr[:, hd + HEADS, :] = qh

    xin_b = xinb_scr[pl.ds(r0, sb), :]
    acc = cbw_ref[CONV_B_K - 1:CONV_B_K, :] * xin_b
    for k in range(CONV_B_K - 1):
        acc = acc + cbw_ref[k:k + 1, :] * stb_ref[k]
    yb_scr[pl.ds(r0, sb), :] = acc
    nstb_ref[0:CONV_B_K - 2] = stb_ref[1:CONV_B_K - 1]
    nstb_ref[CONV_B_K - 2] = xin_b

    xin_c = xinc_scr[pl.ds(r0, sb), :]
    yc_scr[pl.ds(r0, sb), :] = (ccw_ref[0:1, :] * stc_ref[0] + ccw_ref[1:2, :] * stc_ref[1]
                                + ccw_ref[2:3, :] * xin_c)
    nstc_ref[0] = stc_ref[1]
    nstc_ref[1] = xin_c

    ones = jnp.ones((HEAD_DIM, LANES), BF16)
    n_tiles = N_MEM * HEADS // SUBLANES
    for s in range(sb):
        q8 = q_scr[r0 + s]
        prod = (k_ref[s] * q8[None]).reshape(n_tiles * SUBLANES, HEAD_DIM)
        sc = (_dot(prod.astype(BF16), ones) * ATTN_SCALE).reshape(n_tiles, SUBLANES, LANES)
        mx = jnp.max(sc, axis=0)
        mx = jnp.maximum(mx, pltpu.roll(mx, HEADS, axis=0))
        e = jnp.exp(sc - mx[None])
        num = jnp.sum(e * v_ref[s], axis=0)
        den = jnp.sum(e, axis=0)
        num = num + pltpu.roll(num, HEADS, axis=0)
        den = den + pltpu.roll(den, HEADS, axis=0)
        o_scr[r0 + s] = num / den

    @pl.when(i == pl.num_programs(0) - 1)
    def _():
        h = h_scr[...]

        def gate(n):
            z = _dot(h, wg_ref[:, n * D_MODEL:(n + 1) * D_MODEL]) + bg_ref[:, n * D_MODEL:(n + 1) * D_MODEL]
            return jax.nn.sigmoid(z)

        merged = gate(0) * _dot(pa_scr[...].astype(BF16), wa_ref[...])
        yb = jax.nn.silu(_layer_norm(yb_scr[...] + cbb_ref[...], cblg_ref[...], cblb_ref[...]))
        merged = merged + gate(1) * _dot(yb.astype(BF16), wb_ref[...])
        merged = merged + gate(2) * _dot((gatec_scr[...] * yc_scr[...]).astype(BF16), wc_ref[...])
        om = None
        for hd in range(HEADS):
            part = _dot(o_scr[:, hd, :].astype(BF16), wm_ref[hd * HEAD_DIM:(hd + 1) * HEAD_DIM, :])
            om = part if om is None else om + part
        merged = merged + gate(3) * om
        xo_ref[...] = x_ref[...] + _dot(merged.astype(BF16), wo_ref[...])


def _mix_sample_call(x, kc, vc, stb, stc, layer, w):
    n = x.shape[0]
    sb = SAMPLE_BLOCK
    n_tiles = N_MEM * HEADS // SUBLANES
    const = lambda shape: pl.BlockSpec(shape, lambda i: (0,) * len(shape))
    in_specs = [
        const((n, D_MODEL)),
        pl.BlockSpec((None, sb, n_tiles, SUBLANES, HEAD_DIM), lambda i: (layer, i, 0, 0, 0)),
        pl.BlockSpec((None, sb, n_tiles, SUBLANES, HEAD_DIM), lambda i: (layer, i, 0, 0, 0)),
        pl.BlockSpec((None, CONV_B_K - 1, sb, HALF), lambda i: (layer, 0, i, 0)),
        pl.BlockSpec((None, CONV_C_K - 1, sb, HALF), lambda i: (layer, 0, i, 0)),
        _resident((1, D_MODEL), layer),
        _resident((D_MODEL, 8 * HALF), layer),
        _resident((1, HALF), layer),
        _resident((1, HALF), layer),
        _resident((1, HALF), layer),
        _resident((1, HALF), layer),
        _resident((HALF, D_MODEL), layer),
        _resident((CONV_B_K, HALF), layer),
        _resident((1, HALF), layer),
        _resident((1, HALF), layer),
        _resident((1, HALF), layer),
        _resident((HALF, D_MODEL), layer),
        _resident((CONV_C_K, HALF), layer),
        _resident((HALF, D_MODEL), layer),
        _resident((HALF, D_MODEL), layer),
        _resident((D_MODEL, 4 * D_MODEL), layer),
        _resident((1, 4 * D_MODEL), layer),
        _resident((D_MODEL, D_MODEL), layer),
    ]
    out_specs = [
        const((n, D_MODEL)),
        const((n, HALF)),
        pl.BlockSpec((CONV_B_K - 1, sb, HALF), lambda i: (0, i, 0)),
        pl.BlockSpec((CONV_C_K - 1, sb, HALF), lambda i: (0, i, 0)),
    ]
    out_shape = [
        jax.ShapeDtypeStruct((n, D_MODEL), F32),
        jax.ShapeDtypeStruct((n, HALF), F32),
        jax.ShapeDtypeStruct((CONV_B_K - 1, n, HALF), F32),
        jax.ShapeDtypeStruct((CONV_C_K - 1, n, HALF), F32),
    ]
    scratch = [
        pltpu.VMEM((n, D_MODEL), BF16),
        pltpu.VMEM((n, HALF), F32),
        pltpu.VMEM((n, HALF), F32),
        pltpu.VMEM((n, HALF), F32),
        pltpu.VMEM((n, HALF), F32),
        pltpu.VMEM((n, SUBLANES, HEAD_DIM), F32),
        pltpu.VMEM((n, HALF), F32),
        pltpu.VMEM((n, HALF), F32),
        pltpu.VMEM((n, SUBLANES, HEAD_DIM), F32),
    ]
    return pl.pallas_call(
        _mix_sample_kernel,
        grid=(n // sb,),
        in_specs=in_specs,
        out_specs=out_specs,
        out_shape=out_shape,
        scratch_shapes=scratch,
        compiler_params=pltpu.CompilerParams(
            dimension_semantics=("arbitrary",), vmem_limit_bytes=VMEM_LIMIT),
        name=f"mix_sample_{layer}",
    )(x, kc, vc, stb, stc, w["mix_norm"], w["w_in"], w["gmlp_ln_g"], w["gmlp_ln_b"], w["gmlp_ws0"],
      w["gmlp_bs0"], w["gmlp_w_out"], w["conv_b_w"], w["conv_b_bias"], w["conv_b_ln_g"],
      w["conv_b_ln_b"], w["conv_b_w_out"], w["conv_c_w"], w["conv_c_w_out"], w["mem_w_out"],
      w["w_branch_gate"], w["b_branch_gate"], w["w_o"])


def kernel(x_prompt, x_sample, mem_prompt, state_conv_b, state_conv_c, cache_mem_k, cache_mem_v,
           ffn1_norm, ffn1_w_gate_up, ffn1_w_down, mix_norm, w_in,
           gmlp_ln_g, gmlp_ln_b, gmlp_w_s, gmlp_b_s, gmlp_w_out,
           conv_b_w, conv_b_bias, conv_b_ln_g, conv_b_ln_b, conv_b_w_out,
           conv_c_w, conv_c_w_out, mem_norm, mem_w_k, mem_w_v, mem_w_out,
           w_branch_gate, b_branch_gate, w_o, ffn2_norm, ffn2_w_gate_up, ffn2_w_down, final_norm):
    batch, seq, _ = x_prompt.shape
    n_sample = x_sample.shape[0]
    row = lambda a: a.reshape(DEPTH, 1, a.shape[-1])

    w = {
        "mix_norm": row(mix_norm), "w_in": w_in.astype(BF16),
        "gmlp_ln_g": row(gmlp_ln_g), "gmlp_ln_b": row(gmlp_ln_b),
        "gmlp_w_s": gmlp_w_s, "gmlp_b_s_t": jnp.swapaxes(gmlp_b_s, 1, 2),
        "gmlp_ws0": row(jnp.repeat(gmlp_w_s[:, :, 0, 0], LANES, axis=1)),
        "gmlp_bs0": row(jnp.repeat(gmlp_b_s[:, :, 0], LANES, axis=1)),
        "gmlp_w_out": gmlp_w_out.astype(BF16),
        "conv_b_w": conv_b_w, "conv_b_bias": row(conv_b_bias),
        "conv_b_w8": jnp.broadcast_to(conv_b_w[:, :, None, :], (DEPTH, CONV_B_K, SUBLANES, HALF)),
        "conv_b_ln_g": row(conv_b_ln_g), "conv_b_ln_b": row(conv_b_ln_b),
        "conv_b_w_out": conv_b_w_out.astype(BF16),
        "conv_c_w": conv_c_w, "conv_c_w_out": conv_c_w_out.astype(BF16),
        "mem_w_out": mem_w_out.astype(BF16),
        "w_branch_gate": w_branch_gate.astype(BF16), "b_branch_gate": row(b_branch_gate),
        "w_o": w_o.astype(BF16),
    }
    ffn1 = (row(ffn1_norm), ffn1_w_gate_up.astype(BF16), ffn1_w_down.astype(BF16))
    ffn2 = (row(ffn2_norm), ffn2_w_gate_up.astype(BF16), ffn2_w_down.astype(BF16))
    final = final_norm.reshape(1, D_MODEL)

    kmem, vmem_ = _memkv_call(mem_prompt.reshape(batch * N_MEM, D_MODEL), row(mem_norm),
                              mem_w_k.astype(BF16), mem_w_v.astype(BF16))

    n_tiles = N_MEM * HEADS // SUBLANES
    kc = cache_mem_k.reshape(DEPTH, n_sample, n_tiles, SUBLANES, HEAD_DIM)
    vc = cache_mem_v.reshape(DEPTH, n_sample, n_tiles, SUBLANES, HEAD_DIM)

    stb = jnp.swapaxes(state_conv_b, 1, 2)
    stc = jnp.swapaxes(state_conv_c, 1, 2)

    xp = x_prompt.reshape(batch * seq, D_MODEL)
    xs = x_sample.reshape(n_sample, D_MODEL)
    cb_p, cb_s, cc_p, cc_s, gv_s = [], [], [], [], []
    for l in range(DEPTH):
        last = l == DEPTH - 1
        xp, xs = _ffn_call(xp, xs, l, *ffn1, None, name=f"ffn1_{l}")
        xp, tb, tc = _mix_prompt_call(xp, kmem, vmem_, l, w, batch=batch, seq=seq)
        xs, gv, nb, nc = _mix_sample_call(xs, kc, vc, stb, stc, l, w)
        xp, xs = _ffn_call(xp, xs, l, *ffn2, final if last else None, name=f"ffn2_{l}")
        cb_p.append(tb); cc_p.append(tc); cb_s.append(nb); cc_s.append(nc); gv_s.append(gv)

    kv_shape = (DEPTH, batch, N_MEM, HEADS, HEAD_DIM)
    return (xp.reshape(batch, seq, D_MODEL), xs.reshape(n_sample, 1, D_MODEL),
            kmem.reshape(kv_shape), vmem_.reshape(kv_shape),
            jnp.stack(cb_p), jnp.swapaxes(jnp.stack(cb_s), 1, 2),
            jnp.stack(cc_p), jnp.swapaxes(jnp.stack(cc_s), 1, 2),
            jnp.stack(gv_s).reshape(DEPTH, n_sample, 1, HALF))
```

```python
import functools

import jax
import jax.numpy as jnp
from jax import lax
from jax.experimental import pallas as pl
from jax.experimental.pallas import tpu as pltpu

F32 = jnp.float32
BF16 = jnp.bfloat16

D_MODEL = 1024
DEPTH = 4
N_MEM = 256
CHUNK = 128
GROUPS = 4
HALF = 512
CONV_B_K = 31
CONV_C_K = 3
HEADS = 4
HEAD_DIM = 128
D_FF = 2816
EPS = 1e-6
ATTN_SCALE = HEAD_DIM ** -0.5

SUBLANES = 8
LANES = 128
BF16_TILE_ROWS = 16

FF_CHUNK = 256
N_FF_CHUNKS = D_FF // FF_CHUNK
TM_FFN = 512
TM_MIX = 512
CONV_ROWS = 32
HALO_B = 32
HALO_C = 8
SAMPLE_BLOCK = 8
VMEM_LIMIT = 56 * 1024 * 1024


def _dot(a, b):
    return jnp.dot(a, b, preferred_element_type=F32)


def _rms(x, g):
    y = x * lax.rsqrt(jnp.mean(x * x, axis=-1, keepdims=True) + EPS)
    return y * g


def _gelu(x):
    return 0.5 * x * (1.0 + lax.erf(x * (0.5 ** 0.5)))


def _layer_norm(x, g, b):
    mu = jnp.mean(x, axis=-1, keepdims=True)
    xc = x - mu
    var = jnp.mean(xc * xc, axis=-1, keepdims=True)
    return xc * lax.rsqrt(var + EPS) * g + b


def _cast_specs(arrays, layer, n_steps, step_of):
    in_specs, out_specs, out_shapes = [], [], []
    for a in arrays:
        rows, cols = a.shape[1:]
        n_blocks = n_steps
        while rows % (n_blocks * BF16_TILE_ROWS):
            n_blocks //= 2
        per = n_steps // n_blocks
        rb = rows // n_blocks
        in_specs.append(pl.BlockSpec((None, rb, cols), lambda *g, per=per: (layer, step_of(*g) // per, 0)))
        out_specs.append(pl.BlockSpec((rb, cols), lambda *g, per=per: (step_of(*g) // per, 0)))
        out_shapes.append(jax.ShapeDtypeStruct((rows, cols), BF16))
    return in_specs, out_specs, out_shapes


def _cast_blocks(in_refs, out_refs):
    for i_ref, o_ref in zip(in_refs, out_refs):
        o_ref[...] = i_ref[...].astype(BF16)


def _ffn_kernel(*refs, final, n_cast):
    n_in = 5 + int(final)
    xp_ref, xs_ref, g_ref, wgu_ref, wd_ref = refs[:5]
    fg_ref = refs[5] if final else None
    cast_in = refs[n_in:n_in + n_cast]
    op_ref, os_ref = refs[n_in + n_cast:n_in + n_cast + 2]
    cast_out = refs[n_in + n_cast + 2:]
    _cast_blocks(cast_in, cast_out)

    def rows(x_ref, o_ref):
        x = x_ref[...]
        h = _rms(x, g_ref[...]).astype(BF16)
        acc = None
        for c in range(N_FF_CHUNKS):
            gate = _dot(h, wgu_ref[:, c * FF_CHUNK:(c + 1) * FF_CHUNK])
            up = _dot(h, wgu_ref[:, D_FF + c * FF_CHUNK:D_FF + (c + 1) * FF_CHUNK])
            a = (jax.nn.silu(gate) * up).astype(BF16)
            d = _dot(a, wd_ref[c * FF_CHUNK:(c + 1) * FF_CHUNK, :])
            acc = d if acc is None else acc + d
        y = x + 0.5 * acc
        if final:
            y = _rms(y, fg_ref[...])
        o_ref[...] = y

    rows(xp_ref, op_ref)

    @pl.when(pl.program_id(0) == pl.num_programs(0) - 1)
    def _():
        rows(xs_ref, os_ref)


def _resident(shape, layer=None):
    nd = len(shape)
    if layer is None:
        return pl.BlockSpec(tuple(shape), lambda *_: (0,) * nd, pipeline_mode=pl.Buffered(1))
    return pl.BlockSpec((None,) + tuple(shape), lambda *_: (layer,) + (0,) * nd,
                        pipeline_mode=pl.Buffered(1))


def _ffn_call(xp, xs, layer, norm, wgu, wd, final_norm, cast, cast_layer, *, name):
    m, n = xp.shape[0], xs.shape[0]
    tm = TM_FFN
    n_steps = m // tm
    final = final_norm is not None
    sample_spec = pl.BlockSpec((n, D_MODEL), lambda i: (0, 0))
    in_specs = [
        pl.BlockSpec((tm, D_MODEL), lambda i: (i, 0)),
        sample_spec,
        _resident((1, D_MODEL), layer),
        _resident((D_MODEL, 2 * D_FF)),
        _resident((D_FF, D_MODEL)),
    ]
    args = [xp, xs, norm, wgu, wd]
    if final:
        in_specs.append(pl.BlockSpec((1, D_MODEL), lambda i: (0, 0)))
        args.append(final_norm)
    cast_in, cast_out, cast_shapes = _cast_specs(cast, cast_layer, n_steps, lambda i: i)
    return pl.pallas_call(
        functools.partial(_ffn_kernel, final=final, n_cast=len(cast)),
        grid=(n_steps,),
        in_specs=in_specs + cast_in,
        out_specs=[pl.BlockSpec((tm, D_MODEL), lambda i: (i, 0)), sample_spec] + cast_out,
        out_shape=[jax.ShapeDtypeStruct((m, D_MODEL), F32),
                   jax.ShapeDtypeStruct((n, D_MODEL), F32)] + cast_shapes,
        compiler_params=pltpu.CompilerParams(
            dimension_semantics=("arbitrary",), vmem_limit_bytes=VMEM_LIMIT),
        name=name,
    )(*args, *cast)


def _memkv_kernel(m_ref, g_ref, wk_ref, wv_ref, k_ref, v_ref):
    m = _rms(m_ref[...], g_ref[...]).astype(BF16)
    k_ref[...] = _dot(m, wk_ref[...])
    v_ref[...] = _dot(m, wv_ref[...])


def _memkv_call(mem, norm, wk, wv):
    rows = mem.shape[0]
    tm = 512
    per_layer = lambda shape: pl.BlockSpec((None,) + shape, lambda l, i: (l, 0, 0))
    out_spec = pl.BlockSpec((None, tm, HALF), lambda l, i: (l, i, 0))
    return pl.pallas_call(
        _memkv_kernel,
        grid=(DEPTH, rows // tm),
        in_specs=[pl.BlockSpec((tm, D_MODEL), lambda l, i: (i, 0)),
                  per_layer((1, D_MODEL)), per_layer((D_MODEL, HALF)), per_layer((D_MODEL, HALF))],
        out_specs=[out_spec, out_spec],
        out_shape=[jax.ShapeDtypeStruct((DEPTH, rows, HALF), F32)] * 2,
        compiler_params=pltpu.CompilerParams(
            dimension_semantics=("arbitrary", "arbitrary"), vmem_limit_bytes=VMEM_LIMIT),
        name="memkv",
    )(mem, norm, wk, wv)


def _causal_conv(ext_ref, w_ref, taps, first_row, rows):
    outs = []
    for r in range(rows // CONV_ROWS):
        base = first_row + r * CONV_ROWS
        acc = None
        for k in range(taps):
            term = w_ref[k:k + 1, :] * ext_ref[base + k:base + k + CONV_ROWS, :]
            acc = term if acc is None else acc + term
        outs.append(acc)
    return jnp.concatenate(outs, axis=0)


def _shifted_copies(ext_ref, shifted_ref):
    total = ext_ref.shape[0]
    tiles = ext_ref[...].reshape(total // SUBLANES, SUBLANES, HALF)
    sub = lax.broadcasted_iota(jnp.int32, (1, SUBLANES, HALF), 1)
    for s in range(1, SUBLANES):
        rot = pltpu.roll(tiles, SUBLANES - s, axis=1)
        nxt = jnp.concatenate([rot[1:], rot[:1]], axis=0)
        shifted_ref[s - 1] = jnp.where(sub < SUBLANES - s, rot, nxt).reshape(total, HALF)


def _conv_chunks(ext_ref, shifted_ref, w_ref, out_ref, taps, first_row, chunks):
    for r in chunks:
        acc = None
        for k in range(taps):
            s = (first_row + k) % SUBLANES
            base = r * CONV_ROWS + first_row + k - s
            src = ext_ref if s == 0 else shifted_ref.at[s - 1]
            win = src[base:base + CONV_ROWS, :].reshape(CONV_ROWS // SUBLANES, SUBLANES, HALF)
            term = w_ref[k][None] * win
            acc = term if acc is None else acc + term
        out_ref[r * CONV_ROWS:(r + 1) * CONV_ROWS, :] = acc.reshape(CONV_ROWS, HALF)


def _mix_prompt_kernel(x_ref, k_ref, v_ref, norm_ref, win_ref, lng_ref, lnb_ref, ws_ref, bst_ref,
                       wa_ref, cbw_ref, cbb_ref, cblg_ref, cblb_ref, wb_ref, ccw_ref, wc_ref, wm_ref,
                       wg_ref, bg_ref, wo_ref, cast_gu_ref, cast_d_ref,
                       xo_ref, tailb_ref, tailc_ref, wgu_out_ref, wd_out_ref,
                       extb_ref, extc_ref, shiftb_ref, h_scr, yb_scr, merged_scr):
    _cast_blocks((cast_gu_ref, cast_d_ref), (wgu_out_ref, wd_out_ref))
    tm = x_ref.shape[0]
    j = pl.program_id(1)
    n_conv = tm // CONV_ROWS
    conv_split = (0, 6 * n_conv // 16, 11 * n_conv // 16, n_conv)

    def gate(i):
        z = _dot(h_scr[...], wg_ref[:, i * D_MODEL:(i + 1) * D_MODEL])
        return jax.nn.sigmoid(z + bg_ref[:, i * D_MODEL:(i + 1) * D_MODEL])

    def conv_b(part):
        _conv_chunks(extb_ref, shiftb_ref, cbw_ref, yb_scr, CONV_B_K, HALO_B - (CONV_B_K - 1),
                     range(conv_split[part], conv_split[part + 1]))

    @pl.when(j == 0)
    def _():
        extb_ref[0:HALO_B, :] = jnp.zeros((HALO_B, HALF), F32)
        extc_ref[0:HALO_C, :] = jnp.zeros((HALO_C, HALF), F32)

    h = _rms(x_ref[...], norm_ref[...]).astype(BF16)
    h_scr[...] = h

    zb = _dot(h, win_ref[:, 2 * HALF:4 * HALF])
    extb_ref[HALO_B:HALO_B + tm, :] = zb[:, :HALF] * jax.nn.sigmoid(zb[:, HALF:])
    _shifted_copies(extb_ref, shiftb_ref)

    conv_b(0)
    za = _gelu(_dot(h_scr[...], win_ref[:, 0:2 * HALF]))
    u = za[:, :HALF]
    v = _layer_norm(za[:, HALF:], lng_ref[...], lnb_ref[...]).astype(BF16)
    n_chunks = tm // CHUNK
    row = lax.broadcasted_iota(jnp.int32, (CHUNK, CHUNK), 0)
    col = lax.broadcasted_iota(jnp.int32, (CHUNK, CHUNK), 1)
    pieces = [[None] * GROUPS for _ in range(n_chunks)]
    for g in range(GROUPS):
        ws = jnp.where(row >= col, ws_ref[g], 0.0).astype(BF16)
        rhs = jnp.concatenate(
            [v[c * CHUNK:(c + 1) * CHUNK, g * LANES:(g + 1) * LANES] for c in range(n_chunks)], axis=1)
        res = _dot(ws, rhs) + bst_ref[:, g:g + 1]
        for c in range(n_chunks):
            pieces[c][g] = res[:, c * CHUNK:(c + 1) * CHUNK]
    mixed = jnp.concatenate([jnp.concatenate(p, axis=1) for p in pieces], axis=0)
    merged_scr[...] = gate(0) * _dot((u * mixed).astype(BF16), wa_ref[...])

    conv_b(1)
    zc = _dot(h_scr[...], win_ref[:, 4 * HALF:7 * HALF])
    extc_ref[HALO_C:HALO_C + tm, :] = zc[:, HALF:2 * HALF] * zc[:, 2 * HALF:]
    yc = _causal_conv(extc_ref, ccw_ref, CONV_C_K, HALO_C - (CONV_C_K - 1), tm)
    merged_scr[...] += gate(2) * _dot((zc[:, :HALF] * yc).astype(BF16), wc_ref[...])

    conv_b(2)
    q = _dot(h_scr[...], win_ref[:, 7 * HALF:8 * HALF])
    kb = k_ref[...].astype(BF16)
    vb = v_ref[...].astype(BF16)
    ones = jnp.ones((N_MEM, HEAD_DIM), BF16)
    heads = []
    for hd in range(HEADS):
        sl = slice(hd * HEAD_DIM, (hd + 1) * HEAD_DIM)
        s = lax.dot_general(q[:, sl].astype(BF16), kb[:, sl], (((1,), (1,)), ((), ())),
                            preferred_element_type=F32) * ATTN_SCALE
        e = jnp.exp(s - jnp.max(s, axis=-1, keepdims=True)).astype(BF16)
        r = _dot(e, jnp.concatenate([vb[:, sl], ones], axis=1))
        heads.append(r[:, :HEAD_DIM] / r[:, HEAD_DIM:])
    om = jnp.concatenate(heads, axis=1).astype(BF16)
    merged_scr[...] += gate(3) * _dot(om, wm_ref[...])

    yb = yb_scr[...] + cbb_ref[...]
    yb = jax.nn.silu(_layer_norm(yb, cblg_ref[...], cblb_ref[...]))
    merged = merged_scr[...] + gate(1) * _dot(yb.astype(BF16), wb_ref[...])
    xo_ref[...] = x_ref[...] + _dot(merged.astype(BF16), wo_ref[...])

    @pl.when(j == pl.num_programs(1) - 1)
    def _():
        tailb_ref[...] = extb_ref[tm + HALO_B - (CONV_B_K - 1):tm + HALO_B, :]
        tailc_ref[...] = extc_ref[tm + HALO_C - (CONV_C_K - 1):tm + HALO_C, :]

    extb_ref[0:HALO_B, :] = extb_ref[tm:tm + HALO_B, :]
    extc_ref[0:HALO_C, :] = extc_ref[tm:tm + HALO_C, :]


def _mix_prompt_call(x, kmem, vmem_, layer, w, mw, cast, *, batch, seq):
    tm = TM_MIX
    nj = seq // tm
    cast_in, cast_out, cast_shapes = _cast_specs(cast, layer, batch * nj, lambda b, j: b * nj + j)
    in_specs = [
        pl.BlockSpec((tm, D_MODEL), lambda b, j: (b * nj + j, 0)),
        pl.BlockSpec((None, N_MEM, HALF), lambda b, j: (layer, b, 0)),
        pl.BlockSpec((None, N_MEM, HALF), lambda b, j: (layer, b, 0)),
        _resident((1, D_MODEL), layer),
        _resident((D_MODEL, 8 * HALF)),
        _resident((1, HALF), layer),
        _resident((1, HALF), layer),
        _resident((GROUPS, CHUNK, CHUNK), layer),
        _resident((CHUNK, GROUPS), layer),
        _resident((HALF, D_MODEL)),
        _resident((CONV_B_K, SUBLANES, HALF), layer),
        _resident((1, HALF), layer),
        _resident((1, HALF), layer),
        _resident((1, HALF), layer),
        _resident((HALF, D_MODEL)),
        _resident((CONV_C_K, HALF), layer),
        _resident((HALF, D_MODEL)),
        _resident((HALF, D_MODEL)),
        _resident((D_MODEL, 4 * D_MODEL)),
        _resident((1, 4 * D_MODEL), layer),
        _resident((D_MODEL, D_MODEL)),
    ]
    out_specs = [
        pl.BlockSpec((tm, D_MODEL), lambda b, j: (b * nj + j, 0)),
        pl.BlockSpec((None, CONV_B_K - 1, HALF), lambda b, j: (b, 0, 0)),
        pl.BlockSpec((None, CONV_C_K - 1, HALF), lambda b, j: (b, 0, 0)),
    ]
    out_shape = [
        jax.ShapeDtypeStruct((batch * seq, D_MODEL), F32),
        jax.ShapeDtypeStruct((batch, CONV_B_K - 1, HALF), F32),
        jax.ShapeDtypeStruct((batch, CONV_C_K - 1, HALF), F32),
    ]
    return pl.pallas_call(
        _mix_prompt_kernel,
        grid=(batch, nj),
        in_specs=in_specs + cast_in,
        out_specs=out_specs + cast_out,
        out_shape=out_shape + cast_shapes,
        scratch_shapes=[pltpu.VMEM((HALO_B + tm, HALF), F32), pltpu.VMEM((HALO_C + tm, HALF), F32),
                        pltpu.VMEM((SUBLANES - 1, HALO_B + tm, HALF), F32),
                        pltpu.VMEM((tm, D_MODEL), BF16),
                        pltpu.VMEM((tm, HALF), F32),
                        pltpu.VMEM((tm, D_MODEL), F32)],
        compiler_params=pltpu.CompilerParams(
            dimension_semantics=("arbitrary", "arbitrary"), vmem_limit_bytes=VMEM_LIMIT),
        name=f"mix_prompt_{layer}",
    )(x, kmem, vmem_, w["mix_norm"], mw["w_in"], w["gmlp_ln_g"], w["gmlp_ln_b"], w["gmlp_w_s"],
      w["gmlp_b_s_t"], mw["gmlp_w_out"], w["conv_b_w8"], w["conv_b_bias"], w["conv_b_ln_g"],
      w["conv_b_ln_b"], mw["conv_b_w_out"], w["conv_c_w"], mw["conv_c_w_out"], mw["mem_w_out"],
      mw["w_branch_gate"], w["b_branch_gate"], mw["w_o"], *cast)


def _mix_sample_kernel(x_ref, k_ref, v_ref, stb_ref, stc_ref, norm_ref, win_ref, lng_ref, lnb_ref,
                       ws0_ref, bs0_ref, wa_ref, cbw_ref, cbb_ref, cblg_ref, cblb_ref, wb_ref, ccw_ref,
                       wc_ref, wm_ref, wg_ref, bg_ref, wo_ref,
                       xo_ref, gv_ref, nstb_ref, nstc_ref,
                       h_scr, pa_scr, xinb_scr, gatec_scr, xinc_scr, q_scr, yb_scr, yc_scr, o_scr):
    i = pl.program_id(0)
    sb = stb_ref.shape[1]
    r0 = pl.multiple_of(i * sb, sb)

    @pl.when(i == 0)
    def _():
        h = _rms(x_ref[...], norm_ref[...]).astype(BF16)
        h_scr[...] = h
        za = _gelu(_dot(h, win_ref[:, 0:2 * HALF]))
        v = _layer_norm(za[:, HALF:], lng_ref[...], lnb_ref[...])
        gv_ref[...] = v
        pa_scr[...] = za[:, :HALF] * (ws0_ref[...] * v + bs0_ref[...])
        zb = _dot(h, win_ref[:, 2 * HALF:4 * HALF])
        xinb_scr[...] = zb[:, :HALF] * jax.nn.sigmoid(zb[:, HALF:])
        zc = _dot(h, win_ref[:, 4 * HALF:7 * HALF])
        gatec_scr[...] = zc[:, :HALF]
        xinc_scr[...] = zc[:, HALF:2 * HALF] * zc[:, 2 * HALF:]
        q = _dot(h, win_ref[:, 7 * HALF:8 * HALF])
        for hd in range(HEADS):
            qh = q[:, hd * HEAD_DIM:(hd + 1) * HEAD_DIM]
            q_scr[:, hd, :] = qh
            q_scr[:, hd + HEADS, :] = qh

    xin_b = xinb_scr[pl.ds(r0, sb), :]
    acc = cbw_ref[CONV_B_K - 1:CONV_B_K, :] * xin_b
    for k in range(CONV_B_K - 1):
        acc = acc + cbw_ref[k:k + 1, :] * stb_ref[k]
    yb_scr[pl.ds(r0, sb), :] = acc
    nstb_ref[0:CONV_B_K - 2] = stb_ref[1:CONV_B_K - 1]
    nstb_ref[CONV_B_K - 2] = xin_b

    xin_c = xinc_scr[pl.ds(r0, sb), :]
    yc_scr[pl.ds(r0, sb), :] = (ccw_ref[0:1, :] * stc_ref[0] + ccw_ref[1:2, :] * stc_ref[1]
                                + ccw_ref[2:3, :] * xin_c)
    nstc_ref[0] = stc_ref[1]
    nstc_ref[1] = xin_c

    ones = jnp.ones((HEAD_DIM, LANES), BF16)
    n_tiles = N_MEM * HEADS // SUBLANES
    for s in range(sb):
        q8 = q_scr[r0 + s]
        prod = (k_ref[s] * q8[None]).reshape(n_tiles * SUBLANES, HEAD_DIM)
        sc = (_dot(prod.astype(BF16), ones) * ATTN_SCALE).reshape(n_tiles, SUBLANES, LANES)
        mx = jnp.max(sc, axis=0)
        mx = jnp.maximum(mx, pltpu.roll(mx, HEADS, axis=0))
        e = jnp.exp(sc - mx[None])
        num = jnp.sum(e * v_ref[s], axis=0)
        den = jnp.sum(e, axis=0)
        num = num + pltpu.roll(num, HEADS, axis=0)
        den = den + pltpu.roll(den, HEADS, axis=0)
        o_scr[r0 + s] = num / den

    @pl.when(i == pl.num_programs(0) - 1)
    def _():
        h = h_scr[...]

        def gate(n):
            z = _dot(h, wg_ref[:, n * D_MODEL:(n + 1) * D_MODEL]) + bg_ref[:, n * D_MODEL:(n + 1) * D_MODEL]
            return jax.nn.sigmoid(z)

        merged = gate(0) * _dot(pa_scr[...].astype(BF16), wa_ref[...])
        yb = jax.nn.silu(_layer_norm(yb_scr[...] + cbb_ref[...], cblg_ref[...], cblb_ref[...]))
        merged = merged + gate(1) * _dot(yb.astype(BF16), wb_ref[...])
        merged = merged + gate(2) * _dot((gatec_scr[...] * yc_scr[...]).astype(BF16), wc_ref[...])
        om = None
        for hd in range(HEADS):
            part = _dot(o_scr[:, hd, :].astype(BF16), wm_ref[hd * HEAD_DIM:(hd + 1) * HEAD_DIM, :])
            om = part if om is None else om + part
        merged = merged + gate(3) * om
        xo_ref[...] = x_ref[...] + _dot(merged.astype(BF16), wo_ref[...])


def _mix_sample_call(x, kc, vc, stb, stc, layer, w, mw):
    n = x.shape[0]
    sb = SAMPLE_BLOCK
    n_tiles = N_MEM * HEADS // SUBLANES
    const = lambda shape: pl.BlockSpec(shape, lambda i: (0,) * len(shape))
    in_specs = [
        const((n, D_MODEL)),
        pl.BlockSpec((None, sb, n_tiles, SUBLANES, HEAD_DIM), lambda i: (layer, i, 0, 0, 0)),
        pl.BlockSpec((None, sb, n_tiles, SUBLANES, HEAD_DIM), lambda i: (layer, i, 0, 0, 0)),
        pl.BlockSpec((None, CONV_B_K - 1, sb, HALF), lambda i: (layer, 0, i, 0)),
        pl.BlockSpec((None, CONV_C_K - 1, sb, HALF), lambda i: (layer, 0, i, 0)),
        _resident((1, D_MODEL), layer),
        _resident((D_MODEL, 8 * HALF)),
        _resident((1, HALF), layer),
        _resident((1, HALF), layer),
        _resident((1, HALF), layer),
        _resident((1, HALF), layer),
        _resident((HALF, D_MODEL)),
        _resident((CONV_B_K, HALF), layer),
        _resident((1, HALF), layer),
        _resident((1, HALF), layer),
        _resident((1, HALF), layer),
        _resident((HALF, D_MODEL)),
        _resident((CONV_C_K, HALF), layer),
        _resident((HALF, D_MODEL)),
        _resident((HALF, D_MODEL)),
        _resident((D_MODEL, 4 * D_MODEL)),
        _resident((1, 4 * D_MODEL), layer),
        _resident((D_MODEL, D_MODEL)),
    ]
    out_specs = [
        const((n, D_MODEL)),
        const((n, HALF)),
        pl.BlockSpec((CONV_B_K - 1, sb, HALF), lambda i: (0, i, 0)),
        pl.BlockSpec((CONV_C_K - 1, sb, HALF), lambda i: (0, i, 0)),
    ]
    out_shape = [
        jax.ShapeDtypeStruct((n, D_MODEL), F32),
        jax.ShapeDtypeStruct((n, HALF), F32),
        jax.ShapeDtypeStruct((CONV_B_K - 1, n, HALF), F32),
        jax.ShapeDtypeStruct((CONV_C_K - 1, n, HALF), F32),
    ]
    scratch = [
        pltpu.VMEM((n, D_MODEL), BF16),
        pltpu.VMEM((n, HALF), F32),
        pltpu.VMEM((n, HALF), F32),
        pltpu.VMEM((n, HALF), F32),
        pltpu.VMEM((n, HALF), F32),
        pltpu.VMEM((n, SUBLANES, HEAD_DIM), F32),
        pltpu.VMEM((n, HALF), F32),
        pltpu.VMEM((n, HALF), F32),
        pltpu.VMEM((n, SUBLANES, HEAD_DIM), F32),
    ]
    return pl.pallas_call(
        _mix_sample_kernel,
        grid=(n // sb,),
        in_specs=in_specs,
        out_specs=out_specs,
        out_shape=out_shape,
        scratch_shapes=scratch,
        compiler_params=pltpu.CompilerParams(
            dimension_semantics=("arbitrary",), vmem_limit_bytes=VMEM_LIMIT),
        name=f"mix_sample_{layer}",
    )(x, kc, vc, stb, stc, w["mix_norm"], mw["w_in"], w["gmlp_ln_g"], w["gmlp_ln_b"], w["gmlp_ws0"],
      w["gmlp_bs0"], mw["gmlp_w_out"], w["conv_b_w"], w["conv_b_bias"], w["conv_b_ln_g"],
      w["conv_b_ln_b"], mw["conv_b_w_out"], w["conv_c_w"], mw["conv_c_w_out"], mw["mem_w_out"],
      mw["w_branch_gate"], w["b_branch_gate"], mw["w_o"])


def kernel(x_prompt, x_sample, mem_prompt, state_conv_b, state_conv_c, cache_mem_k, cache_mem_v,
           ffn1_norm, ffn1_w_gate_up, ffn1_w_down, mix_norm, w_in,
           gmlp_ln_g, gmlp_ln_b, gmlp_w_s, gmlp_b_s, gmlp_w_out,
           conv_b_w, conv_b_bias, conv_b_ln_g, conv_b_ln_b, conv_b_w_out,
           conv_c_w, conv_c_w_out, mem_norm, mem_w_k, mem_w_v, mem_w_out,
           w_branch_gate, b_branch_gate, w_o, ffn2_norm, ffn2_w_gate_up, ffn2_w_down, final_norm):
    batch, seq, _ = x_prompt.shape
    n_sample = x_sample.shape[0]
    row = lambda a: a.reshape(DEPTH, 1, a.shape[-1])

    w = {
        "mix_norm": row(mix_norm),
        "gmlp_ln_g": row(gmlp_ln_g), "gmlp_ln_b": row(gmlp_ln_b),
        "gmlp_w_s": gmlp_w_s, "gmlp_b_s_t": jnp.swapaxes(gmlp_b_s, 1, 2),
        "gmlp_ws0": row(jnp.repeat(gmlp_w_s[:, :, 0, 0], LANES, axis=1)),
        "gmlp_bs0": row(jnp.repeat(gmlp_b_s[:, :, 0], LANES, axis=1)),
        "conv_b_w": conv_b_w, "conv_b_bias": row(conv_b_bias),
        "conv_b_w8": jnp.broadcast_to(conv_b_w[:, :, None, :], (DEPTH, CONV_B_K, SUBLANES, HALF)),
        "conv_b_ln_g": row(conv_b_ln_g), "conv_b_ln_b": row(conv_b_ln_b),
        "conv_c_w": conv_c_w,
        "b_branch_gate": row(b_branch_gate),
    }
    mix_names = ("w_in", "w_branch_gate", "w_o", "gmlp_w_out", "conv_b_w_out", "conv_c_w_out", "mem_w_out")
    mix_f32 = [w_in, w_branch_gate, w_o, gmlp_w_out, conv_b_w_out, conv_c_w_out, mem_w_out]
    ffn1_f32 = [ffn1_w_gate_up, ffn1_w_down]
    ffn2_f32 = [ffn2_w_gate_up, ffn2_w_down]
    ffn1_norm, ffn2_norm = row(ffn1_norm), row(ffn2_norm)
    final = final_norm.reshape(1, D_MODEL)

    kmem, vmem_ = _memkv_call(mem_prompt.reshape(batch * N_MEM, D_MODEL), row(mem_norm),
                              mem_w_k.astype(BF16), mem_w_v.astype(BF16))

    n_tiles = N_MEM * HEADS // SUBLANES
    kc = cache_mem_k.reshape(DEPTH, n_sample, n_tiles, SUBLANES, HEAD_DIM)
    vc = cache_mem_v.reshape(DEPTH, n_sample, n_tiles, SUBLANES, HEAD_DIM)

    stb = jnp.swapaxes(state_conv_b, 1, 2)
    stc = jnp.swapaxes(state_conv_c, 1, 2)

    xp = x_prompt.reshape(batch * seq, D_MODEL)
    xs = x_sample.reshape(n_sample, D_MODEL)
    cb_p, cb_s, cc_p, cc_s, gv_s = [], [], [], [], []
    ffn1_w = [a[0].astype(BF16) for a in ffn1_f32]
    for l in range(DEPTH):
        last = l == DEPTH - 1
        xp, xs, *mix_w = _ffn_call(xp, xs, l, ffn1_norm, *ffn1_w, None, mix_f32, l, name=f"ffn1_{l}")
        mw = dict(zip(mix_names, mix_w))
        xp, tb, tc, *ffn2_w = _mix_prompt_call(xp, kmem, vmem_, l, w, mw, ffn2_f32, batch=batch, seq=seq)
        xs, gv, nb, nc = _mix_sample_call(xs, kc, vc, stb, stc, l, w, mw)
        xp, xs, *ffn1_w = _ffn_call(xp, xs, l, ffn2_norm, *ffn2_w, final if last else None,
                                    [] if last else ffn1_f32, l + 1, name=f"ffn2_{l}")
        cb_p.append(tb); cc_p.append(tc); cb_s.append(nb); cc_s.append(nc); gv_s.append(gv)

    kv_shape = (DEPTH, batch, N_MEM, HEADS, HEAD_DIM)
    return (xp.reshape(batch, seq, D_MODEL), xs.reshape(n_sample, 1, D_MODEL),
            kmem.reshape(kv_shape), vmem_.reshape(kv_shape),
            jnp.stack(cb_p), jnp.swapaxes(jnp.stack(cb_s), 1, 2),
            jnp.stack(cc_p), jnp.swapaxes(jnp.stack(cc_s), 1, 2),
            jnp.stack(gv_s).reshape(DEPTH, n_sample, 1, HALF))
```

```python
import functools

import jax
import jax.numpy as jnp
from jax import lax
from jax.experimental import pallas as pl
from jax.experimental.pallas import tpu as pltpu

F32 = jnp.float32
BF16 = jnp.bfloat16

D_MODEL = 1024
DEPTH = 4
N_MEM = 256
CHUNK = 128
GROUPS = 4
HALF = 512
CONV_B_K = 31
CONV_C_K = 3
HEADS = 4
HEAD_DIM = 128
D_FF = 2816
EPS = 1e-6
ATTN_SCALE = HEAD_DIM ** -0.5

SUBLANES = 8
LANES = 128
BF16_TILE_ROWS = 16

MXU_TILE = 256
FF_CHUNKS = (512,) * 5 + (256,)
assert sum(FF_CHUNKS) == D_FF and all(c % MXU_TILE == 0 for c in FF_CHUNKS)
TM_FFN = 1024
TM_MIX = 512
CONV_ROWS = 32
HALO_B = 32
HALO_C = 8
SAMPLE_BLOCK = 8
VMEM_LIMIT = 56 * 1024 * 1024


def _dot(a, b):
    return jnp.dot(a, b, preferred_element_type=F32)


def _rms(x, g):
    y = x * lax.rsqrt(jnp.mean(x * x, axis=-1, keepdims=True) + EPS)
    return y * g


def _gelu(x):
    return 0.5 * x * (1.0 + lax.erf(x * (0.5 ** 0.5)))


def _layer_norm(x, g, b):
    mu = jnp.mean(x, axis=-1, keepdims=True)
    xc = x - mu
    var = jnp.mean(xc * xc, axis=-1, keepdims=True)
    return xc * lax.rsqrt(var + EPS) * g + b


def _cast_specs(arrays, layer, n_steps, step_of):
    in_specs, out_specs, out_shapes = [], [], []
    for a in arrays:
        rows, cols = a.shape[1:]
        n_blocks = n_steps
        while rows % (n_blocks * BF16_TILE_ROWS):
            n_blocks //= 2
        per = n_steps // n_blocks
        rb = rows // n_blocks
        in_specs.append(pl.BlockSpec((None, rb, cols), lambda *g, per=per: (layer, step_of(*g) // per, 0)))
        out_specs.append(pl.BlockSpec((rb, cols), lambda *g, per=per: (step_of(*g) // per, 0)))
        out_shapes.append(jax.ShapeDtypeStruct((rows, cols), BF16))
    return in_specs, out_specs, out_shapes


def _cast_blocks(in_refs, out_refs):
    for i_ref, o_ref in zip(in_refs, out_refs):
        o_ref[...] = i_ref[...].astype(BF16)


def _ffn_kernel(*refs, final, n_cast):
    n_in = 5 + int(final)
    xp_ref, xs_ref, g_ref, wgu_ref, wd_ref = refs[:5]
    fg_ref = refs[5] if final else None
    cast_in = refs[n_in:n_in + n_cast]
    op_ref, os_ref = refs[n_in + n_cast:n_in + n_cast + 2]
    cast_out = refs[n_in + n_cast + 2:]
    _cast_blocks(cast_in, cast_out)

    def rows(x_ref, o_ref):
        x = x_ref[...]
        h = _rms(x, g_ref[...]).astype(BF16)
        acc = None
        lo = 0
        for width in FF_CHUNKS:
            gate = _dot(h, wgu_ref[:, lo:lo + width])
            up = _dot(h, wgu_ref[:, D_FF + lo:D_FF + lo + width])
            a = (jax.nn.silu(gate) * up).astype(BF16)
            d = _dot(a, wd_ref[lo:lo + width, :])
            acc = d if acc is None else acc + d
            lo += width
        y = x + 0.5 * acc
        if final:
            y = _rms(y, fg_ref[...])
        o_ref[...] = y

    rows(xp_ref, op_ref)

    @pl.when(pl.program_id(0) == pl.num_programs(0) - 1)
    def _():
        rows(xs_ref, os_ref)


def _resident(shape, layer=None):
    nd = len(shape)
    if layer is None:
        return pl.BlockSpec(tuple(shape), lambda *_: (0,) * nd, pipeline_mode=pl.Buffered(1))
    return pl.BlockSpec((None,) + tuple(shape), lambda *_: (layer,) + (0,) * nd,
                        pipeline_mode=pl.Buffered(1))


def _ffn_call(xp, xs, layer, norm, wgu, wd, final_norm, cast, cast_layer, *, name):
    m, n = xp.shape[0], xs.shape[0]
    tm = TM_FFN
    n_steps = m // tm
    final = final_norm is not None
    sample_spec = pl.BlockSpec((n, D_MODEL), lambda i: (0, 0))
    in_specs = [
        pl.BlockSpec((tm, D_MODEL), lambda i: (i, 0)),
        sample_spec,
        _resident((1, D_MODEL), layer),
        _resident((D_MODEL, 2 * D_FF)),
        _resident((D_FF, D_MODEL)),
    ]
    args = [xp, xs, norm, wgu, wd]
    if final:
        in_specs.append(pl.BlockSpec((1, D_MODEL), lambda i: (0, 0)))
        args.append(final_norm)
    cast_in, cast_out, cast_shapes = _cast_specs(cast, cast_layer, n_steps, lambda i: i)
    return pl.pallas_call(
        functools.partial(_ffn_kernel, final=final, n_cast=len(cast)),
        grid=(n_steps,),
        in_specs=in_specs + cast_in,
        out_specs=[pl.BlockSpec((tm, D_MODEL), lambda i: (i, 0)), sample_spec] + cast_out,
        out_shape=[jax.ShapeDtypeStruct((m, D_MODEL), F32),
                   jax.ShapeDtypeStruct((n, D_MODEL), F32)] + cast_shapes,
        compiler_params=pltpu.CompilerParams(
            dimension_semantics=("arbitrary",), vmem_limit_bytes=VMEM_LIMIT),
        name=name,
    )(*args, *cast)


def _memkv_kernel(m_ref, g_ref, wk_ref, wv_ref, k_ref, v_ref):
    m = _rms(m_ref[...], g_ref[...]).astype(BF16)
    k = _dot(m, wk_ref[...])
    v = _dot(m, wv_ref[...])
    for hd in range(HEADS):
        k_ref[pl.ds(hd, N_MEM, stride=HEADS), :] = k[:, hd * HEAD_DIM:(hd + 1) * HEAD_DIM]
        v_ref[pl.ds(hd, N_MEM, stride=HEADS), :] = v[:, hd * HEAD_DIM:(hd + 1) * HEAD_DIM]


def _memkv_call(mem, norm, wk, wv):
    batch = mem.shape[0] // N_MEM
    tm = N_MEM
    per_layer = lambda shape: pl.BlockSpec((None,) + shape, lambda l, i: (l, 0, 0))
    out_spec = pl.BlockSpec((None, None, N_MEM * HEADS, HEAD_DIM), lambda l, i: (l, i, 0, 0))
    return pl.pallas_call(
        _memkv_kernel,
        grid=(DEPTH, batch),
        in_specs=[pl.BlockSpec((tm, D_MODEL), lambda l, i: (i, 0)),
                  per_layer((1, D_MODEL)), per_layer((D_MODEL, HALF)), per_layer((D_MODEL, HALF))],
        out_specs=[out_spec, out_spec],
        out_shape=[jax.ShapeDtypeStruct((DEPTH, batch, N_MEM * HEADS, HEAD_DIM), F32)] * 2,
        compiler_params=pltpu.CompilerParams(
            dimension_semantics=("arbitrary", "arbitrary"), vmem_limit_bytes=VMEM_LIMIT),
        name="memkv",
    )(mem, norm, wk, wv)


def _causal_conv(ext_ref, w_ref, taps, first_row, rows):
    outs = []
    for r in range(rows // CONV_ROWS):
        base = first_row + r * CONV_ROWS
        acc = None
        for k in range(taps):
            term = w_ref[k:k + 1, :] * ext_ref[base + k:base + k + CONV_ROWS, :]
            acc = term if acc is None else acc + term
        outs.append(acc)
    return jnp.concatenate(outs, axis=0)


def _shifted_copies(ext_ref, shifted_ref):
    total = ext_ref.shape[0]
    tiles = ext_ref[...].reshape(total // SUBLANES, SUBLANES, HALF)
    sub = lax.broadcasted_iota(jnp.int32, (1, SUBLANES, HALF), 1)
    for s in range(1, SUBLANES):
        rot = pltpu.roll(tiles, SUBLANES - s, axis=1)
        nxt = jnp.concatenate([rot[1:], rot[:1]], axis=0)
        shifted_ref[s - 1] = jnp.where(sub < SUBLANES - s, rot, nxt).reshape(total, HALF)


def _conv_chunks(ext_ref, shifted_ref, w_ref, out_ref, taps, first_row, chunks):
    for r in chunks:
        acc = None
        for k in range(taps):
            s = (first_row + k) % SUBLANES
            base = r * CONV_ROWS + first_row + k - s
            src = ext_ref if s == 0 else shifted_ref.at[s - 1]
            win = src[base:base + CONV_ROWS, :].reshape(CONV_ROWS // SUBLANES, SUBLANES, HALF)
            term = w_ref[k][None] * win
            acc = term if acc is None else acc + term
        out_ref[r * CONV_ROWS:(r + 1) * CONV_ROWS, :] = acc.reshape(CONV_ROWS, HALF)


def _mix_prompt_kernel(x_ref, k_ref, v_ref, norm_ref, win_ref, lng_ref, lnb_ref, ws_ref, bst_ref,
                       wa_ref, cbw_ref, cbb_ref, cblg_ref, cblb_ref, wb_ref, ccw_ref, wc_ref, wm_ref,
                       wg_ref, bg_ref, wo_ref, cast_gu_ref, cast_d_ref,
                       xo_ref, tailb_ref, tailc_ref, wgu_out_ref, wd_out_ref,
                       extb_ref, extc_ref, shiftb_ref, h_scr, yb_scr, merged_scr):
    _cast_blocks((cast_gu_ref, cast_d_ref), (wgu_out_ref, wd_out_ref))
    tm = x_ref.shape[0]
    j = pl.program_id(1)
    n_conv = tm // CONV_ROWS
    conv_split = (0, 6 * n_conv // 16, 11 * n_conv // 16, n_conv)

    def gate(i):
        z = _dot(h_scr[...], wg_ref[:, i * D_MODEL:(i + 1) * D_MODEL])
        return jax.nn.sigmoid(z + bg_ref[:, i * D_MODEL:(i + 1) * D_MODEL])

    def conv_b(part):
        _conv_chunks(extb_ref, shiftb_ref, cbw_ref, yb_scr, CONV_B_K, HALO_B - (CONV_B_K - 1),
                     range(conv_split[part], conv_split[part + 1]))

    @pl.when(j == 0)
    def _():
        extb_ref[0:HALO_B, :] = jnp.zeros((HALO_B, HALF), F32)
        extc_ref[0:HALO_C, :] = jnp.zeros((HALO_C, HALF), F32)

    h = _rms(x_ref[...], norm_ref[...]).astype(BF16)
    h_scr[...] = h

    zb = _dot(h, win_ref[:, 2 * HALF:4 * HALF])
    extb_ref[HALO_B:HALO_B + tm, :] = zb[:, :HALF] * jax.nn.sigmoid(zb[:, HALF:])
    _shifted_copies(extb_ref, shiftb_ref)

    conv_b(0)
    za = _gelu(_dot(h_scr[...], win_ref[:, 0:2 * HALF]))
    u = za[:, :HALF]
    v = _layer_norm(za[:, HALF:], lng_ref[...], lnb_ref[...]).astype(BF16)
    n_chunks = tm // CHUNK
    row = lax.broadcasted_iota(jnp.int32, (CHUNK, CHUNK), 0)
    col = lax.broadcasted_iota(jnp.int32, (CHUNK, CHUNK), 1)
    pieces = [[None] * GROUPS for _ in range(n_chunks)]
    for g in range(GROUPS):
        ws = jnp.where(row >= col, ws_ref[g], 0.0).astype(BF16)
        rhs = jnp.concatenate(
            [v[c * CHUNK:(c + 1) * CHUNK, g * LANES:(g + 1) * LANES] for c in range(n_chunks)], axis=1)
        res = _dot(ws, rhs) + bst_ref[:, g:g + 1]
        for c in range(n_chunks):
            pieces[c][g] = res[:, c * CHUNK:(c + 1) * CHUNK]
    mixed = jnp.concatenate([jnp.concatenate(p, axis=1) for p in pieces], axis=0)
    merged_scr[...] = gate(0) * _dot((u * mixed).astype(BF16), wa_ref[...])

    conv_b(1)
    zc = _dot(h_scr[...], win_ref[:, 4 * HALF:7 * HALF])
    extc_ref[HALO_C:HALO_C + tm, :] = zc[:, HALF:2 * HALF] * zc[:, 2 * HALF:]
    yc = _causal_conv(extc_ref, ccw_ref, CONV_C_K, HALO_C - (CONV_C_K - 1), tm)
    merged_scr[...] += gate(2) * _dot((zc[:, :HALF] * yc).astype(BF16), wc_ref[...])

    conv_b(2)
    q = _dot(h_scr[...], win_ref[:, 7 * HALF:8 * HALF])
    ones = jnp.ones((N_MEM, HEAD_DIM), BF16)
    heads = []
    for hd in range(HEADS):
        sl = slice(hd * HEAD_DIM, (hd + 1) * HEAD_DIM)
        kh = k_ref[pl.ds(hd, N_MEM, stride=HEADS), :].astype(BF16)
        vh = v_ref[pl.ds(hd, N_MEM, stride=HEADS), :].astype(BF16)
        s = lax.dot_general(q[:, sl].astype(BF16), kh, (((1,), (1,)), ((), ())),
                            preferred_element_type=F32) * ATTN_SCALE
        e = jnp.exp(s - jnp.max(s, axis=-1, keepdims=True)).astype(BF16)
        r = _dot(e, jnp.concatenate([vh, ones], axis=1))
        heads.append(r[:, :HEAD_DIM] / r[:, HEAD_DIM:])
    om = jnp.concatenate(heads, axis=1).astype(BF16)
    merged_scr[...] += gate(3) * _dot(om, wm_ref[...])

    yb = yb_scr[...] + cbb_ref[...]
    yb = jax.nn.silu(_layer_norm(yb, cblg_ref[...], cblb_ref[...]))
    merged = merged_scr[...] + gate(1) * _dot(yb.astype(BF16), wb_ref[...])
    xo_ref[...] = x_ref[...] + _dot(merged.astype(BF16), wo_ref[...])

    @pl.when(j == pl.num_programs(1) - 1)
    def _():
        tailb_ref[...] = extb_ref[tm + HALO_B - (CONV_B_K - 1):tm + HALO_B, :]
        tailc_ref[...] = extc_ref[tm + HALO_C - (CONV_C_K - 1):tm + HALO_C, :]

    extb_ref[0:HALO_B, :] = extb_ref[tm:tm + HALO_B, :]
    extc_ref[0:HALO_C, :] = extc_ref[tm:tm + HALO_C, :]


def _mix_prompt_call(x, kmem, vmem_, layer, w, mw, cast, *, batch, seq):
    tm = TM_MIX
    nj = seq // tm
    cast_in, cast_out, cast_shapes = _cast_specs(cast, layer, batch * nj, lambda b, j: b * nj + j)
    in_specs = [
        pl.BlockSpec((tm, D_MODEL), lambda b, j: (b * nj + j, 0)),
        pl.BlockSpec((None, None, N_MEM * HEADS, HEAD_DIM), lambda b, j: (layer, b, 0, 0)),
        pl.BlockSpec((None, None, N_MEM * HEADS, HEAD_DIM), lambda b, j: (layer, b, 0, 0)),
        _resident((1, D_MODEL), layer),
        _resident((D_MODEL, 8 * HALF)),
        _resident((1, HALF), layer),
        _resident((1, HALF), layer),
        _resident((GROUPS, CHUNK, CHUNK), layer),
        _resident((CHUNK, GROUPS), layer),
        _resident((HALF, D_MODEL)),
        _resident((CONV_B_K, SUBLANES, HALF), layer),
        _resident((1, HALF), layer),
        _resident((1, HALF), layer),
        _resident((1, HALF), layer),
        _resident((HALF, D_MODEL)),
        _resident((CONV_C_K, HALF), layer),
        _resident((HALF, D_MODEL)),
        _resident((HALF, D_MODEL)),
        _resident((D_MODEL, 4 * D_MODEL)),
        _resident((1, 4 * D_MODEL), layer),
        _resident((D_MODEL, D_MODEL)),
    ]
    out_specs = [
        pl.BlockSpec((tm, D_MODEL), lambda b, j: (b * nj + j, 0)),
        pl.BlockSpec((None, CONV_B_K - 1, HALF), lambda b, j: (b, 0, 0)),
        pl.BlockSpec((None, CONV_C_K - 1, HALF), lambda b, j: (b, 0, 0)),
    ]
    out_shape = [
        jax.ShapeDtypeStruct((batch * seq, D_MODEL), F32),
        jax.ShapeDtypeStruct((batch, CONV_B_K - 1, HALF), F32),
        jax.ShapeDtypeStruct((batch, CONV_C_K - 1, HALF), F32),
    ]
    return pl.pallas_call(
        _mix_prompt_kernel,
        grid=(batch, nj),
        in_specs=in_specs + cast_in,
        out_specs=out_specs + cast_out,
        out_shape=out_shape + cast_shapes,
        scratch_shapes=[pltpu.VMEM((HALO_B + tm, HALF), F32), pltpu.VMEM((HALO_C + tm, HALF), F32),
                        pltpu.VMEM((SUBLANES - 1, HALO_B + tm, HALF), F32),
                        pltpu.VMEM((tm, D_MODEL), BF16),
                        pltpu.VMEM((tm, HALF), F32),
                        pltpu.VMEM((tm, D_MODEL), F32)],
        compiler_params=pltpu.CompilerParams(
            dimension_semantics=("arbitrary", "arbitrary"), vmem_limit_bytes=VMEM_LIMIT),
        name=f"mix_prompt_{layer}",
    )(x, kmem, vmem_, w["mix_norm"], mw["w_in"], w["gmlp_ln_g"], w["gmlp_ln_b"], w["gmlp_w_s"],
      w["gmlp_b_s_t"], mw["gmlp_w_out"], w["conv_b_w8"], w["conv_b_bias"], w["conv_b_ln_g"],
      w["conv_b_ln_b"], mw["conv_b_w_out"], w["conv_c_w"], mw["conv_c_w_out"], mw["mem_w_out"],
      mw["w_branch_gate"], w["b_branch_gate"], mw["w_o"], *cast)


def _mix_sample_kernel(x_ref, k_ref, v_ref, stb_ref, stc_ref, norm_ref, win_ref, lng_ref, lnb_ref,
                       ws0_ref, bs0_ref, wa_ref, cbw_ref, cbb_ref, cblg_ref, cblb_ref, wb_ref, ccw_ref,
                       wc_ref, wm_ref, wg_ref, bg_ref, wo_ref,
                       xo_ref, gv_ref, nstb_ref, nstc_ref,
                       h_scr, pa_scr, xinb_scr, gatec_scr, xinc_scr, q_scr, yb_scr, yc_scr, o_scr):
    i = pl.program_id(0)
    sb = stb_ref.shape[1]
    r0 = pl.multiple_of(i * sb, sb)

    @pl.when(i == 0)
    def _():
        h = _rms(x_ref[...], norm_ref[...]).astype(BF16)
        h_scr[...] = h
        za = _gelu(_dot(h, win_ref[:, 0:2 * HALF]))
        v = _layer_norm(za[:, HALF:], lng_ref[...], lnb_ref[...])
        gv_ref[...] = v
        pa_scr[...] = za[:, :HALF] * (ws0_ref[...] * v + bs0_ref[...])
        zb = _dot(h, win_ref[:, 2 * HALF:4 * HALF])
        xinb_scr[...] = zb[:, :HALF] * jax.nn.sigmoid(zb[:, HALF:])
        zc = _dot(h, win_ref[:, 4 * HALF:7 * HALF])
        gatec_scr[...] = zc[:, :HALF]
        xinc_scr[...] = zc[:, HALF:2 * HALF] * zc[:, 2 * HALF:]
        q = _dot(h, win_ref[:, 7 * HALF:8 * HALF])
        for hd in range(HEADS):
            qh = q[:, hd * HEAD_DIM:(hd + 1) * HEAD_DIM]
            q_scr[:, hd, :] = qh
            q_scr[:, hd + HEADS, :] = qh

    xin_b = xinb_scr[pl.ds(r0, sb), :]
    acc = cbw_ref[CONV_B_K - 1:CONV_B_K, :] * xin_b
    for k in range(CONV_B_K - 1):
        acc = acc + cbw_ref[k:k + 1, :] * stb_ref[k]
    yb_scr[pl.ds(r0, sb), :] = acc
    nstb_ref[0:CONV_B_K - 2] = stb_ref[1:CONV_B_K - 1]
    nstb_ref[CONV_B_K - 2] = xin_b

    xin_c = xinc_scr[pl.ds(r0, sb), :]
    yc_scr[pl.ds(r0, sb), :] = (ccw_ref[0:1, :] * stc_ref[0] + ccw_ref[1:2, :] * stc_ref[1]
                                + ccw_ref[2:3, :] * xin_c)
    nstc_ref[0] = stc_ref[1]
    nstc_ref[1] = xin_c

    ones = jnp.ones((HEAD_DIM, LANES), BF16)
    n_tiles = N_MEM * HEADS // SUBLANES
    for s in range(sb):
        q8 = q_scr[r0 + s]
        prod = (k_ref[s] * q8[None]).reshape(n_tiles * SUBLANES, HEAD_DIM)
        sc = (_dot(prod.astype(BF16), ones) * ATTN_SCALE).reshape(n_tiles, SUBLANES, LANES)
        mx = jnp.max(sc, axis=0)
        mx = jnp.maximum(mx, pltpu.roll(mx, HEADS, axis=0))
        e = jnp.exp(sc - mx[None])
        num = jnp.sum(e * v_ref[s], axis=0)
        den = jnp.sum(e, axis=0)
        num = num + pltpu.roll(num, HEADS, axis=0)
        den = den + pltpu.roll(den, HEADS, axis=0)
        o_scr[r0 + s] = num / den

    @pl.when(i == pl.num_programs(0) - 1)
    def _():
        h = h_scr[...]

        def gate(n):
            z = _dot(h, wg_ref[:, n * D_MODEL:(n + 1) * D_MODEL]) + bg_ref[:, n * D_MODEL:(n + 1) * D_MODEL]
            return jax.nn.sigmoid(z)

        merged = gate(0) * _dot(pa_scr[...].astype(BF16), wa_ref[...])
        yb = jax.nn.silu(_layer_norm(yb_scr[...] + cbb_ref[...], cblg_ref[...], cblb_ref[...]))
        merged = merged + gate(1) * _dot(yb.astype(BF16), wb_ref[...])
        merged = merged + gate(2) * _dot((gatec_scr[...] * yc_scr[...]).astype(BF16), wc_ref[...])
        om = None
        for hd in range(HEADS):
            part = _dot(o_scr[:, hd, :].astype(BF16), wm_ref[hd * HEAD_DIM:(hd + 1) * HEAD_DIM, :])
            om = part if om is None else om + part
        merged = merged + gate(3) * om
        xo_ref[...] = x_ref[...] + _dot(merged.astype(BF16), wo_ref[...])


def _mix_sample_call(x, kc, vc, stb, stc, layer, w, mw):
    n = x.shape[0]
    sb = SAMPLE_BLOCK
    n_tiles = N_MEM * HEADS // SUBLANES
    const = lambda shape: pl.BlockSpec(shape, lambda i: (0,) * len(shape))
    in_specs = [
        const((n, D_MODEL)),
        pl.BlockSpec((None, sb, n_tiles, SUBLANES, HEAD_DIM), lambda i: (layer, i, 0, 0, 0)),
        pl.BlockSpec((None, sb, n_tiles, SUBLANES, HEAD_DIM), lambda i: (layer, i, 0, 0, 0)),
        pl.BlockSpec((None, CONV_B_K - 1, sb, HALF), lambda i: (layer, 0, i, 0)),
        pl.BlockSpec((None, CONV_C_K - 1, sb, HALF), lambda i: (layer, 0, i, 0)),
        _resident((1, D_MODEL), layer),
        _resident((D_MODEL, 8 * HALF)),
        _resident((1, HALF), layer),
        _resident((1, HALF), layer),
        _resident((1, HALF), layer),
        _resident((1, HALF), layer),
        _resident((HALF, D_MODEL)),
        _resident((CONV_B_K, HALF), layer),
        _resident((1, HALF), layer),
        _resident((1, HALF), layer),
        _resident((1, HALF), layer),
        _resident((HALF, D_MODEL)),
        _resident((CONV_C_K, HALF), layer),
        _resident((HALF, D_MODEL)),
        _resident((HALF, D_MODEL)),
        _resident((D_MODEL, 4 * D_MODEL)),
        _resident((1, 4 * D_MODEL), layer),
        _resident((D_MODEL, D_MODEL)),
    ]
    out_specs = [
        const((n, D_MODEL)),
        const((n, HALF)),
        pl.BlockSpec((CONV_B_K - 1, sb, HALF), lambda i: (0, i, 0)),
        pl.BlockSpec((CONV_C_K - 1, sb, HALF), lambda i: (0, i, 0)),
    ]
    out_shape = [
        jax.ShapeDtypeStruct((n, D_MODEL), F32),
        jax.ShapeDtypeStruct((n, HALF), F32),
        jax.ShapeDtypeStruct((CONV_B_K - 1, n, HALF), F32),
        jax.ShapeDtypeStruct((CONV_C_K - 1, n, HALF), F32),
    ]
    scratch = [
        pltpu.VMEM((n, D_MODEL), BF16),
        pltpu.VMEM((n, HALF), F32),
        pltpu.VMEM((n, HALF), F32),
        pltpu.VMEM((n, HALF), F32),
        pltpu.VMEM((n, HALF), F32),
        pltpu.VMEM((n, SUBLANES, HEAD_DIM), F32),
        pltpu.VMEM((n, HALF), F32),
        pltpu.VMEM((n, HALF), F32),
        pltpu.VMEM((n, SUBLANES, HEAD_DIM), F32),
    ]
    return pl.pallas_call(
        _mix_sample_kernel,
        grid=(n // sb,),
        in_specs=in_specs,
        out_specs=out_specs,
        out_shape=out_shape,
        scratch_shapes=scratch,
        compiler_params=pltpu.CompilerParams(
            dimension_semantics=("arbitrary",), vmem_limit_bytes=VMEM_LIMIT),
        name=f"mix_sample_{layer}",
    )(x, kc, vc, stb, stc, w["mix_norm"], mw["w_in"], w["gmlp_ln_g"], w["gmlp_ln_b"], w["gmlp_ws0"],
      w["gmlp_bs0"], mw["gmlp_w_out"], w["conv_b_w"], w["conv_b_bias"], w["conv_b_ln_g"],
      w["conv_b_ln_b"], mw["conv_b_w_out"], w["conv_c_w"], mw["conv_c_w_out"], mw["mem_w_out"],
      mw["w_branch_gate"], w["b_branch_gate"], mw["w_o"])


def kernel(x_prompt, x_sample, mem_prompt, state_conv_b, state_conv_c, cache_mem_k, cache_mem_v,
           ffn1_norm, ffn1_w_gate_up, ffn1_w_down, mix_norm, w_in,
           gmlp_ln_g, gmlp_ln_b, gmlp_w_s, gmlp_b_s, gmlp_w_out,
           conv_b_w, conv_b_bias, conv_b_ln_g, conv_b_ln_b, conv_b_w_out,
           conv_c_w, conv_c_w_out, mem_norm, mem_w_k, mem_w_v, mem_w_out,
           w_branch_gate, b_branch_gate, w_o, ffn2_norm, ffn2_w_gate_up, ffn2_w_down, final_norm):
    batch, seq, _ = x_prompt.shape
    n_sample = x_sample.shape[0]
    row = lambda a: a.reshape(DEPTH, 1, a.shape[-1])

    w = {
        "mix_norm": row(mix_norm),
        "gmlp_ln_g": row(gmlp_ln_g), "gmlp_ln_b": row(gmlp_ln_b),
        "gmlp_w_s": gmlp_w_s, "gmlp_b_s_t": jnp.swapaxes(gmlp_b_s, 1, 2),
        "gmlp_ws0": row(jnp.repeat(gmlp_w_s[:, :, 0, 0], LANES, axis=1)),
        "gmlp_bs0": row(jnp.repeat(gmlp_b_s[:, :, 0], LANES, axis=1)),
        "conv_b_w": conv_b_w, "conv_b_bias": row(conv_b_bias),
        "conv_b_w8": jnp.broadcast_to(conv_b_w[:, :, None, :], (DEPTH, CONV_B_K, SUBLANES, HALF)),
        "conv_b_ln_g": row(conv_b_ln_g), "conv_b_ln_b": row(conv_b_ln_b),
        "conv_c_w": conv_c_w,
        "b_branch_gate": row(b_branch_gate),
    }
    mix_names = ("w_in", "w_branch_gate", "w_o", "gmlp_w_out", "conv_b_w_out", "conv_c_w_out", "mem_w_out")
    mix_f32 = [w_in, w_branch_gate, w_o, gmlp_w_out, conv_b_w_out, conv_c_w_out, mem_w_out]
    ffn1_f32 = [ffn1_w_gate_up, ffn1_w_down]
    ffn2_f32 = [ffn2_w_gate_up, ffn2_w_down]
    ffn1_norm, ffn2_norm = row(ffn1_norm), row(ffn2_norm)
    final = final_norm.reshape(1, D_MODEL)

    kmem, vmem_ = _memkv_call(mem_prompt.reshape(batch * N_MEM, D_MODEL), row(mem_norm),
                              mem_w_k.astype(BF16), mem_w_v.astype(BF16))

    n_tiles = N_MEM * HEADS // SUBLANES
    kc = cache_mem_k.reshape(DEPTH, n_sample, n_tiles, SUBLANES, HEAD_DIM)
    vc = cache_mem_v.reshape(DEPTH, n_sample, n_tiles, SUBLANES, HEAD_DIM)

    stb = jnp.swapaxes(state_conv_b, 1, 2)
    stc = jnp.swapaxes(state_conv_c, 1, 2)

    xp = x_prompt.reshape(batch * seq, D_MODEL)
    xs = x_sample.reshape(n_sample, D_MODEL)
    cb_p, cb_s, cc_p, cc_s, gv_s = [], [], [], [], []
    ffn1_w = [a[0].astype(BF16) for a in ffn1_f32]
    for l in range(DEPTH):
        last = l == DEPTH - 1
        xp, xs, *mix_w = _ffn_call(xp, xs, l, ffn1_norm, *ffn1_w, None, mix_f32, l, name=f"ffn1_{l}")
        mw = dict(zip(mix_names, mix_w))
        xp, tb, tc, *ffn2_w = _mix_prompt_call(xp, kmem, vmem_, l, w, mw, ffn2_f32, batch=batch, seq=seq)
        xs, gv, nb, nc = _mix_sample_call(xs, kc, vc, stb, stc, l, w, mw)
        xp, xs, *ffn1_w = _ffn_call(xp, xs, l, ffn2_norm, *ffn2_w, final if last else None,
                                    [] if last else ffn1_f32, l + 1, name=f"ffn2_{l}")
        cb_p.append(tb); cc_p.append(tc); cb_s.append(nb); cc_s.append(nc); gv_s.append(gv)

    kv_shape = (DEPTH, batch, N_MEM, HEADS, HEAD_DIM)
    return (xp.reshape(batch, seq, D_MODEL), xs.reshape(n_sample, 1, D_MODEL),
            kmem.reshape(kv_shape), vmem_.reshape(kv_shape),
            jnp.stack(cb_p), jnp.swapaxes(jnp.stack(cb_s), 1, 2),
            jnp.stack(cc_p), jnp.swapaxes(jnp.stack(cc_s), 1, 2),
            jnp.stack(gv_s).reshape(DEPTH, n_sample, 1, HALF))
```

```python
import functools

import jax
import jax.numpy as jnp
from jax import lax
from jax.experimental import pallas as pl
from jax.experimental.pallas import tpu as pltpu

F32 = jnp.float32
BF16 = jnp.bfloat16

D_MODEL = 1024
DEPTH = 4
N_MEM = 256
CHUNK = 128
GROUPS = 4
HALF = 512
CONV_B_K = 31
CONV_C_K = 3
HEADS = 4
HEAD_DIM = 128
D_FF = 2816
EPS = 1e-6
ATTN_SCALE = HEAD_DIM ** -0.5

SUBLANES = 8
LANES = 128
BF16_TILE_ROWS = 16

MXU_TILE = 256
FF_CHUNKS = (1024, 1024, 768)
assert sum(FF_CHUNKS) == D_FF and all(c % MXU_TILE == 0 for c in FF_CHUNKS)
TM_FFN = 1024
TM_MIX = 512
CONV_ROWS = 32
HALO_B = 32
HALO_C = 8
SAMPLE_BLOCK = 8
VMEM_LIMIT = 56 * 1024 * 1024


def _dot(a, b):
    return jnp.dot(a, b, preferred_element_type=F32)


def _rms(x, g):
    y = x * lax.rsqrt(jnp.mean(x * x, axis=-1, keepdims=True) + EPS)
    return y * g


def _gelu(x):
    return 0.5 * x * (1.0 + lax.erf(x * (0.5 ** 0.5)))


def _layer_norm(x, g, b):
    mu = jnp.mean(x, axis=-1, keepdims=True)
    xc = x - mu
    var = jnp.mean(xc * xc, axis=-1, keepdims=True)
    return xc * lax.rsqrt(var + EPS) * g + b


def _cast_specs(arrays, layer, n_steps, step_of):
    in_specs, out_specs, out_shapes = [], [], []
    for a in arrays:
        rows, cols = a.shape[1:]
        n_blocks = n_steps
        while rows % (n_blocks * BF16_TILE_ROWS):
            n_blocks //= 2
        per = n_steps // n_blocks
        rb = rows // n_blocks
        in_specs.append(pl.BlockSpec((None, rb, cols), lambda *g, per=per: (layer, step_of(*g) // per, 0)))
        out_specs.append(pl.BlockSpec((rb, cols), lambda *g, per=per: (step_of(*g) // per, 0)))
        out_shapes.append(jax.ShapeDtypeStruct((rows, cols), BF16))
    return in_specs, out_specs, out_shapes


def _cast_blocks(in_refs, out_refs):
    for i_ref, o_ref in zip(in_refs, out_refs):
        o_ref[...] = i_ref[...].astype(BF16)


def _ffn_kernel(*refs, final, n_cast):
    n_in = 5 + int(final)
    xp_ref, xs_ref, g_ref, wgu_ref, wd_ref = refs[:5]
    fg_ref = refs[5] if final else None
    cast_in = refs[n_in:n_in + n_cast]
    op_ref, os_ref = refs[n_in + n_cast:n_in + n_cast + 2]
    cast_out = refs[n_in + n_cast + 2:]
    _cast_blocks(cast_in, cast_out)

    def rows(x_ref, o_ref):
        x = x_ref[...]
        h = _rms(x, g_ref[...]).astype(BF16)
        acc = None
        lo = 0
        for width in FF_CHUNKS:
            gate = _dot(h, wgu_ref[:, lo:lo + width])
            up = _dot(h, wgu_ref[:, D_FF + lo:D_FF + lo + width])
            a = (jax.nn.silu(gate) * up).astype(BF16)
            d = _dot(a, wd_ref[lo:lo + width, :])
            acc = d if acc is None else acc + d
            lo += width
        y = x + 0.5 * acc
        if final:
            y = _rms(y, fg_ref[...])
        o_ref[...] = y

    rows(xp_ref, op_ref)

    @pl.when(pl.program_id(0) == pl.num_programs(0) - 1)
    def _():
        rows(xs_ref, os_ref)


def _resident(shape, layer=None):
    nd = len(shape)
    if layer is None:
        return pl.BlockSpec(tuple(shape), lambda *_: (0,) * nd, pipeline_mode=pl.Buffered(1))
    return pl.BlockSpec((None,) + tuple(shape), lambda *_: (layer,) + (0,) * nd,
                        pipeline_mode=pl.Buffered(1))


def _ffn_call(xp, xs, layer, norm, wgu, wd, final_norm, cast, cast_layer, *, name):
    m, n = xp.shape[0], xs.shape[0]
    tm = TM_FFN
    n_steps = m // tm
    final = final_norm is not None
    sample_spec = pl.BlockSpec((n, D_MODEL), lambda i: (0, 0))
    in_specs = [
        pl.BlockSpec((tm, D_MODEL), lambda i: (i, 0)),
        sample_spec,
        _resident((1, D_MODEL), layer),
        _resident((D_MODEL, 2 * D_FF)),
        _resident((D_FF, D_MODEL)),
    ]
    args = [xp, xs, norm, wgu, wd]
    if final:
        in_specs.append(pl.BlockSpec((1, D_MODEL), lambda i: (0, 0)))
        args.append(final_norm)
    cast_in, cast_out, cast_shapes = _cast_specs(cast, cast_layer, n_steps, lambda i: i)
    return pl.pallas_call(
        functools.partial(_ffn_kernel, final=final, n_cast=len(cast)),
        grid=(n_steps,),
        in_specs=in_specs + cast_in,
        out_specs=[pl.BlockSpec((tm, D_MODEL), lambda i: (i, 0)), sample_spec] + cast_out,
        out_shape=[jax.ShapeDtypeStruct((m, D_MODEL), F32),
                   jax.ShapeDtypeStruct((n, D_MODEL), F32)] + cast_shapes,
        compiler_params=pltpu.CompilerParams(
            dimension_semantics=("arbitrary",), vmem_limit_bytes=VMEM_LIMIT),
        name=name,
    )(*args, *cast)


def _memkv_kernel(m_ref, g_ref, wk_ref, wv_ref, k_ref, v_ref):
    m = _rms(m_ref[...], g_ref[...]).astype(BF16)
    k = _dot(m, wk_ref[...])
    v = _dot(m, wv_ref[...])
    for hd in range(HEADS):
        k_ref[pl.ds(hd, N_MEM, stride=HEADS), :] = k[:, hd * HEAD_DIM:(hd + 1) * HEAD_DIM]
        v_ref[pl.ds(hd, N_MEM, stride=HEADS), :] = v[:, hd * HEAD_DIM:(hd + 1) * HEAD_DIM]


def _memkv_call(mem, norm, wk, wv):
    batch = mem.shape[0] // N_MEM
    tm = N_MEM
    per_layer = lambda shape: pl.BlockSpec((None,) + shape, lambda l, i: (l, 0, 0))
    out_spec = pl.BlockSpec((None, None, N_MEM * HEADS, HEAD_DIM), lambda l, i: (l, i, 0, 0))
    return pl.pallas_call(
        _memkv_kernel,
        grid=(DEPTH, batch),
        in_specs=[pl.BlockSpec((tm, D_MODEL), lambda l, i: (i, 0)),
                  per_layer((1, D_MODEL)), per_layer((D_MODEL, HALF)), per_layer((D_MODEL, HALF))],
        out_specs=[out_spec, out_spec],
        out_shape=[jax.ShapeDtypeStruct((DEPTH, batch, N_MEM * HEADS, HEAD_DIM), F32)] * 2,
        compiler_params=pltpu.CompilerParams(
            dimension_semantics=("arbitrary", "arbitrary"), vmem_limit_bytes=VMEM_LIMIT),
        name="memkv",
    )(mem, norm, wk, wv)


def _causal_conv(ext_ref, w_ref, taps, first_row, rows):
    outs = []
    for r in range(rows // CONV_ROWS):
        base = first_row + r * CONV_ROWS
        acc = None
        for k in range(taps):
            term = w_ref[k:k + 1, :] * ext_ref[base + k:base + k + CONV_ROWS, :]
            acc = term if acc is None else acc + term
        outs.append(acc)
    return jnp.concatenate(outs, axis=0)


def _shifted_copies(ext_ref, shifted_ref):
    total = ext_ref.shape[0]
    tiles = ext_ref[...].reshape(total // SUBLANES, SUBLANES, HALF)
    sub = lax.broadcasted_iota(jnp.int32, (1, SUBLANES, HALF), 1)
    for s in range(1, SUBLANES):
        rot = pltpu.roll(tiles, SUBLANES - s, axis=1)
        nxt = jnp.concatenate([rot[1:], rot[:1]], axis=0)
        shifted_ref[s - 1] = jnp.where(sub < SUBLANES - s, rot, nxt).reshape(total, HALF)


def _conv_chunks(ext_ref, shifted_ref, w_ref, out_ref, taps, first_row, chunks):
    for r in chunks:
        acc = None
        for k in range(taps):
            s = (first_row + k) % SUBLANES
            base = r * CONV_ROWS + first_row + k - s
            src = ext_ref if s == 0 else shifted_ref.at[s - 1]
            win = src[base:base + CONV_ROWS, :].reshape(CONV_ROWS // SUBLANES, SUBLANES, HALF)
            term = w_ref[k][None] * win
            acc = term if acc is None else acc + term
        out_ref[r * CONV_ROWS:(r + 1) * CONV_ROWS, :] = acc.reshape(CONV_ROWS, HALF)


def _mix_prompt_kernel(x_ref, k_ref, v_ref, norm_ref, win_ref, lng_ref, lnb_ref, ws_ref, bst_ref,
                       wa_ref, cbw_ref, cbb_ref, cblg_ref, cblb_ref, wb_ref, ccw_ref, wc_ref, wm_ref,
                       wg_ref, bg_ref, wo_ref, cast_gu_ref, cast_d_ref,
                       xo_ref, tailb_ref, tailc_ref, wgu_out_ref, wd_out_ref,
                       extb_ref, extc_ref, shiftb_ref, h_scr, yb_scr, merged_scr):
    _cast_blocks((cast_gu_ref, cast_d_ref), (wgu_out_ref, wd_out_ref))
    tm = x_ref.shape[0]
    j = pl.program_id(1)
    n_conv = tm // CONV_ROWS
    conv_split = (0, 6 * n_conv // 16, 11 * n_conv // 16, n_conv)

    def gate(i):
        z = _dot(h_scr[...], wg_ref[:, i * D_MODEL:(i + 1) * D_MODEL])
        return jax.nn.sigmoid(z + bg_ref[:, i * D_MODEL:(i + 1) * D_MODEL])

    def conv_b(part):
        _conv_chunks(extb_ref, shiftb_ref, cbw_ref, yb_scr, CONV_B_K, HALO_B - (CONV_B_K - 1),
                     range(conv_split[part], conv_split[part + 1]))

    @pl.when(j == 0)
    def _():
        extb_ref[0:HALO_B, :] = jnp.zeros((HALO_B, HALF), F32)
        extc_ref[0:HALO_C, :] = jnp.zeros((HALO_C, HALF), F32)

    h = _rms(x_ref[...], norm_ref[...]).astype(BF16)
    h_scr[...] = h

    zb = _dot(h, win_ref[:, 2 * HALF:4 * HALF])
    extb_ref[HALO_B:HALO_B + tm, :] = zb[:, :HALF] * jax.nn.sigmoid(zb[:, HALF:])
    _shifted_copies(extb_ref, shiftb_ref)

    conv_b(0)
    za = _gelu(_dot(h_scr[...], win_ref[:, 0:2 * HALF]))
    u = za[:, :HALF]
    v = _layer_norm(za[:, HALF:], lng_ref[...], lnb_ref[...]).astype(BF16)
    n_chunks = tm // CHUNK
    row = lax.broadcasted_iota(jnp.int32, (CHUNK, CHUNK), 0)
    col = lax.broadcasted_iota(jnp.int32, (CHUNK, CHUNK), 1)
    pieces = [[None] * GROUPS for _ in range(n_chunks)]
    for g in range(GROUPS):
        ws = jnp.where(row >= col, ws_ref[g], 0.0).astype(BF16)
        rhs = jnp.concatenate(
            [v[c * CHUNK:(c + 1) * CHUNK, g * LANES:(g + 1) * LANES] for c in range(n_chunks)], axis=1)
        res = _dot(ws, rhs) + bst_ref[:, g:g + 1]
        for c in range(n_chunks):
            pieces[c][g] = res[:, c * CHUNK:(c + 1) * CHUNK]
    mixed = jnp.concatenate([jnp.concatenate(p, axis=1) for p in pieces], axis=0)
    merged_scr[...] = gate(0) * _dot((u * mixed).astype(BF16), wa_ref[...])

    conv_b(1)
    zc = _dot(h_scr[...], win_ref[:, 4 * HALF:7 * HALF])
    extc_ref[HALO_C:HALO_C + tm, :] = zc[:, HALF:2 * HALF] * zc[:, 2 * HALF:]
    yc = _causal_conv(extc_ref, ccw_ref, CONV_C_K, HALO_C - (CONV_C_K - 1), tm)
    merged_scr[...] += gate(2) * _dot((zc[:, :HALF] * yc).astype(BF16), wc_ref[...])

    conv_b(2)
    q = _dot(h_scr[...], win_ref[:, 7 * HALF:8 * HALF])
    ones = jnp.ones((N_MEM, HEAD_DIM), BF16)
    heads = []
    for hd in range(HEADS):
        sl = slice(hd * HEAD_DIM, (hd + 1) * HEAD_DIM)
        kh = k_ref[pl.ds(hd, N_MEM, stride=HEADS), :].astype(BF16)
        vh = v_ref[pl.ds(hd, N_MEM, stride=HEADS), :].astype(BF16)
        s = lax.dot_general(q[:, sl].astype(BF16), kh, (((1,), (1,)), ((), ())),
                            preferred_element_type=F32) * ATTN_SCALE
        e = jnp.exp(s - jnp.max(s, axis=-1, keepdims=True)).astype(BF16)
        r = _dot(e, jnp.concatenate([vh, ones], axis=1))
        heads.append(r[:, :HEAD_DIM] / r[:, HEAD_DIM:])
    om = jnp.concatenate(heads, axis=1).astype(BF16)
    merged_scr[...] += gate(3) * _dot(om, wm_ref[...])

    yb = yb_scr[...] + cbb_ref[...]
    yb = jax.nn.silu(_layer_norm(yb, cblg_ref[...], cblb_ref[...]))
    merged = merged_scr[...] + gate(1) * _dot(yb.astype(BF16), wb_ref[...])
    xo_ref[...] = x_ref[...] + _dot(merged.astype(BF16), wo_ref[...])

    @pl.when(j == pl.num_programs(1) - 1)
    def _():
        tailb_ref[...] = extb_ref[tm + HALO_B - (CONV_B_K - 1):tm + HALO_B, :]
        tailc_ref[...] = extc_ref[tm + HALO_C - (CONV_C_K - 1):tm + HALO_C, :]

    extb_ref[0:HALO_B, :] = extb_ref[tm:tm + HALO_B, :]
    extc_ref[0:HALO_C, :] = extc_ref[tm:tm + HALO_C, :]


def _mix_prompt_call(x, kmem, vmem_, layer, w, mw, cast, *, batch, seq):
    tm = TM_MIX
    nj = seq // tm
    cast_in, cast_out, cast_shapes = _cast_specs(cast, layer, batch * nj, lambda b, j: b * nj + j)
    in_specs = [
        pl.BlockSpec((tm, D_MODEL), lambda b, j: (b * nj + j, 0)),
        pl.BlockSpec((None, None, N_MEM * HEADS, HEAD_DIM), lambda b, j: (layer, b, 0, 0)),
        pl.BlockSpec((None, None, N_MEM * HEADS, HEAD_DIM), lambda b, j: (layer, b, 0, 0)),
        _resident((1, D_MODEL), layer),
        _resident((D_MODEL, 8 * HALF)),
        _resident((1, HALF), layer),
        _resident((1, HALF), layer),
        _resident((GROUPS, CHUNK, CHUNK), layer),
        _resident((CHUNK, GROUPS), layer),
        _resident((HALF, D_MODEL)),
        _resident((CONV_B_K, SUBLANES, HALF), layer),
        _resident((1, HALF), layer),
        _resident((1, HALF), layer),
        _resident((1, HALF), layer),
        _resident((HALF, D_MODEL)),
        _resident((CONV_C_K, HALF), layer),
        _resident((HALF, D_MODEL)),
        _resident((HALF, D_MODEL)),
        _resident((D_MODEL, 4 * D_MODEL)),
        _resident((1, 4 * D_MODEL), layer),
        _resident((D_MODEL, D_MODEL)),
    ]
    out_specs = [
        pl.BlockSpec((tm, D_MODEL), lambda b, j: (b * nj + j, 0)),
        pl.BlockSpec((None, CONV_B_K - 1, HALF), lambda b, j: (b, 0, 0)),
        pl.BlockSpec((None, CONV_C_K - 1, HALF), lambda b, j: (b, 0, 0)),
    ]
    out_shape = [
        jax.ShapeDtypeStruct((batch * seq, D_MODEL), F32),
        jax.ShapeDtypeStruct((batch, CONV_B_K - 1, HALF), F32),
        jax.ShapeDtypeStruct((batch, CONV_C_K - 1, HALF), F32),
    ]
    return pl.pallas_call(
        _mix_prompt_kernel,
        grid=(batch, nj),
        in_specs=in_specs + cast_in,
        out_specs=out_specs + cast_out,
        out_shape=out_shape + cast_shapes,
        scratch_shapes=[pltpu.VMEM((HALO_B + tm, HALF), F32), pltpu.VMEM((HALO_C + tm, HALF), F32),
                        pltpu.VMEM((SUBLANES - 1, HALO_B + tm, HALF), F32),
                        pltpu.VMEM((tm, D_MODEL), BF16),
                        pltpu.VMEM((tm, HALF), F32),
                        pltpu.VMEM((tm, D_MODEL), F32)],
        compiler_params=pltpu.CompilerParams(
            dimension_semantics=("arbitrary", "arbitrary"), vmem_limit_bytes=VMEM_LIMIT),
        name=f"mix_prompt_{layer}",
    )(x, kmem, vmem_, w["mix_norm"], mw["w_in"], w["gmlp_ln_g"], w["gmlp_ln_b"], w["gmlp_w_s"],
      w["gmlp_b_s_t"], mw["gmlp_w_out"], w["conv_b_w8"], w["conv_b_bias"], w["conv_b_ln_g"],
      w["conv_b_ln_b"], mw["conv_b_w_out"], w["conv_c_w"], mw["conv_c_w_out"], mw["mem_w_out"],
      mw["w_branch_gate"], w["b_branch_gate"], mw["w_o"], *cast)


def _mix_sample_kernel(x_ref, k_ref, v_ref, stb_ref, stc_ref, norm_ref, win_ref, lng_ref, lnb_ref,
                       ws0_ref, bs0_ref, wa_ref, cbw_ref, cbb_ref, cblg_ref, cblb_ref, wb_ref, ccw_ref,
                       wc_ref, wm_ref, wg_ref, bg_ref, wo_ref,
                       xo_ref, gv_ref, nstb_ref, nstc_ref,
                       h_scr, pa_scr, xinb_scr, gatec_scr, xinc_scr, q_scr, yb_scr, yc_scr, o_scr):
    i = pl.program_id(0)
    sb = stb_ref.shape[1]
    r0 = pl.multiple_of(i * sb, sb)

    @pl.when(i == 0)
    def _():
        h = _rms(x_ref[...], norm_ref[...]).astype(BF16)
        h_scr[...] = h
        za = _gelu(_dot(h, win_ref[:, 0:2 * HALF]))
        v = _layer_norm(za[:, HALF:], lng_ref[...], lnb_ref[...])
        gv_ref[...] = v
        pa_scr[...] = za[:, :HALF] * (ws0_ref[...] * v + bs0_ref[...])
        zb = _dot(h, win_ref[:, 2 * HALF:4 * HALF])
        xinb_scr[...] = zb[:, :HALF] * jax.nn.sigmoid(zb[:, HALF:])
        zc = _dot(h, win_ref[:, 4 * HALF:7 * HALF])
        gatec_scr[...] = zc[:, :HALF]
        xinc_scr[...] = zc[:, HALF:2 * HALF] * zc[:, 2 * HALF:]
        q = _dot(h, win_ref[:, 7 * HALF:8 * HALF])
        for hd in range(HEADS):
            qh = q[:, hd * HEAD_DIM:(hd + 1) * HEAD_DIM]
            q_scr[:, hd, :] = qh
            q_scr[:, hd + HEADS, :] = qh

    xin_b = xinb_scr[pl.ds(r0, sb), :]
    acc = cbw_ref[CONV_B_K - 1:CONV_B_K, :] * xin_b
    for k in range(CONV_B_K - 1):
        acc = acc + cbw_ref[k:k + 1, :] * stb_ref[k]
    yb_scr[pl.ds(r0, sb), :] = acc
    nstb_ref[0:CONV_B_K - 2] = stb_ref[1:CONV_B_K - 1]
    nstb_ref[CONV_B_K - 2] = xin_b

    xin_c = xinc_scr[pl.ds(r0, sb), :]
    yc_scr[pl.ds(r0, sb), :] = (ccw_ref[0:1, :] * stc_ref[0] + ccw_ref[1:2, :] * stc_ref[1]
                                + ccw_ref[2:3, :] * xin_c)
    nstc_ref[0] = stc_ref[1]
    nstc_ref[1] = xin_c

    ones = jnp.ones((HEAD_DIM, LANES), BF16)
    n_tiles = N_MEM * HEADS // SUBLANES
    for s in range(sb):
        q8 = q_scr[r0 + s]
        prod = (k_ref[s] * q8[None]).reshape(n_tiles * SUBLANES, HEAD_DIM)
        sc = (_dot(prod.astype(BF16), ones) * ATTN_SCALE).reshape(n_tiles, SUBLANES, LANES)
        mx = jnp.max(sc, axis=0)
        mx = jnp.maximum(mx, pltpu.roll(mx, HEADS, axis=0))
        e = jnp.exp(sc - mx[None])
        num = jnp.sum(e * v_ref[s], axis=0)
        den = jnp.sum(e, axis=0)
        num = num + pltpu.roll(num, HEADS, axis=0)
        den = den + pltpu.roll(den, HEADS, axis=0)
        o_scr[r0 + s] = num / den

    @pl.when(i == pl.num_programs(0) - 1)
    def _():
        h = h_scr[...]

        def gate(n):
            z = _dot(h, wg_ref[:, n * D_MODEL:(n + 1) * D_MODEL]) + bg_ref[:, n * D_MODEL:(n + 1) * D_MODEL]
            return jax.nn.sigmoid(z)

        merged = gate(0) * _dot(pa_scr[...].astype(BF16), wa_ref[...])
        yb = jax.nn.silu(_layer_norm(yb_scr[...] + cbb_ref[...], cblg_ref[...], cblb_ref[...]))
        merged = merged + gate(1) * _dot(yb.astype(BF16), wb_ref[...])
        merged = merged + gate(2) * _dot((gatec_scr[...] * yc_scr[...]).astype(BF16), wc_ref[...])
        om = None
        for hd in range(HEADS):
            part = _dot(o_scr[:, hd, :].astype(BF16), wm_ref[hd * HEAD_DIM:(hd + 1) * HEAD_DIM, :])
            om = part if om is None else om + part
        merged = merged + gate(3) * om
        xo_ref[...] = x_ref[...] + _dot(merged.astype(BF16), wo_ref[...])


def _mix_sample_call(x, kc, vc, stb, stc, layer, w, mw):
    n = x.shape[0]
    sb = SAMPLE_BLOCK
    n_tiles = N_MEM * HEADS // SUBLANES
    const = lambda shape: pl.BlockSpec(shape, lambda i: (0,) * len(shape))
    in_specs = [
        const((n, D_MODEL)),
        pl.BlockSpec((None, sb, n_tiles, SUBLANES, HEAD_DIM), lambda i: (layer, i, 0, 0, 0)),
        pl.BlockSpec((None, sb, n_tiles, SUBLANES, HEAD_DIM), lambda i: (layer, i, 0, 0, 0)),
        pl.BlockSpec((None, CONV_B_K - 1, sb, HALF), lambda i: (layer, 0, i, 0)),
        pl.BlockSpec((None, CONV_C_K - 1, sb, HALF), lambda i: (layer, 0, i, 0)),
        _resident((1, D_MODEL), layer),
        _resident((D_MODEL, 8 * HALF)),
        _resident((1, HALF), layer),
        _resident((1, HALF), layer),
        _resident((1, HALF), layer),
        _resident((1, HALF), layer),
        _resident((HALF, D_MODEL)),
        _resident((CONV_B_K, HALF), layer),
        _resident((1, HALF), layer),
        _resident((1, HALF), layer),
        _resident((1, HALF), layer),
        _resident((HALF, D_MODEL)),
        _resident((CONV_C_K, HALF), layer),
        _resident((HALF, D_MODEL)),
        _resident((HALF, D_MODEL)),
        _resident((D_MODEL, 4 * D_MODEL)),
        _resident((1, 4 * D_MODEL), layer),
        _resident((D_MODEL, D_MODEL)),
    ]
    out_specs = [
        const((n, D_MODEL)),
        const((n, HALF)),
        pl.BlockSpec((CONV_B_K - 1, sb, HALF), lambda i: (0, i, 0)),
        pl.BlockSpec((CONV_C_K - 1, sb, HALF), lambda i: (0, i, 0)),
    ]
    out_shape = [
        jax.ShapeDtypeStruct((n, D_MODEL), F32),
        jax.ShapeDtypeStruct((n, HALF), F32),
        jax.ShapeDtypeStruct((CONV_B_K - 1, n, HALF), F32),
        jax.ShapeDtypeStruct((CONV_C_K - 1, n, HALF), F32),
    ]
    scratch = [
        pltpu.VMEM((n, D_MODEL), BF16),
        pltpu.VMEM((n, HALF), F32),
        pltpu.VMEM((n, HALF), F32),
        pltpu.VMEM((n, HALF), F32),
        pltpu.VMEM((n, HALF), F32),
        pltpu.VMEM((n, SUBLANES, HEAD_DIM), F32),
        pltpu.VMEM((n, HALF), F32),
        pltpu.VMEM((n, HALF), F32),
        pltpu.VMEM((n, SUBLANES, HEAD_DIM), F32),
    ]
    return pl.pallas_call(
        _mix_sample_kernel,
        grid=(n // sb,),
        in_specs=in_specs,
        out_specs=out_specs,
        out_shape=out_shape,
        scratch_shapes=scratch,
        compiler_params=pltpu.CompilerParams(
            dimension_semantics=("arbitrary",), vmem_limit_bytes=VMEM_LIMIT),
        name=f"mix_sample_{layer}",
    )(x, kc, vc, stb, stc, w["mix_norm"], mw["w_in"], w["gmlp_ln_g"], w["gmlp_ln_b"], w["gmlp_ws0"],
      w["gmlp_bs0"], mw["gmlp_w_out"], w["conv_b_w"], w["conv_b_bias"], w["conv_b_ln_g"],
      w["conv_b_ln_b"], mw["conv_b_w_out"], w["conv_c_w"], mw["conv_c_w_out"], mw["mem_w_out"],
      mw["w_branch_gate"], w["b_branch_gate"], mw["w_o"])


def kernel(x_prompt, x_sample, mem_prompt, state_conv_b, state_conv_c, cache_mem_k, cache_mem_v,
           ffn1_norm, ffn1_w_gate_up, ffn1_w_down, mix_norm, w_in,
           gmlp_ln_g, gmlp_ln_b, gmlp_w_s, gmlp_b_s, gmlp_w_out,
           conv_b_w, conv_b_bias, conv_b_ln_g, conv_b_ln_b, conv_b_w_out,
           conv_c_w, conv_c_w_out, mem_norm, mem_w_k, mem_w_v, mem_w_out,
           w_branch_gate, b_branch_gate, w_o, ffn2_norm, ffn2_w_gate_up, ffn2_w_down, final_norm):
    batch, seq, _ = x_prompt.shape
    n_sample = x_sample.shape[0]
    row = lambda a: a.reshape(DEPTH, 1, a.shape[-1])

    w = {
        "mix_norm": row(mix_norm),
        "gmlp_ln_g": row(gmlp_ln_g), "gmlp_ln_b": row(gmlp_ln_b),
        "gmlp_w_s": gmlp_w_s, "gmlp_b_s_t": jnp.swapaxes(gmlp_b_s, 1, 2),
        "gmlp_ws0": row(jnp.repeat(gmlp_w_s[:, :, 0, 0], LANES, axis=1)),
        "gmlp_bs0": row(jnp.repeat(gmlp_b_s[:, :, 0], LANES, axis=1)),
        "conv_b_w": conv_b_w, "conv_b_bias": row(conv_b_bias),
        "conv_b_w8": jnp.broadcast_to(conv_b_w[:, :, None, :], (DEPTH, CONV_B_K, SUBLANES, HALF)),
        "conv_b_ln_g": row(conv_b_ln_g), "conv_b_ln_b": row(conv_b_ln_b),
        "conv_c_w": conv_c_w,
        "b_branch_gate": row(b_branch_gate),
    }
    mix_names = ("w_in", "w_branch_gate", "w_o", "gmlp_w_out", "conv_b_w_out", "conv_c_w_out", "mem_w_out")
    mix_f32 = [w_in, w_branch_gate, w_o, gmlp_w_out, conv_b_w_out, conv_c_w_out, mem_w_out]
    ffn1_f32 = [ffn1_w_gate_up, ffn1_w_down]
    ffn2_f32 = [ffn2_w_gate_up, ffn2_w_down]
    ffn1_norm, ffn2_norm = row(ffn1_norm), row(ffn2_norm)
    final = final_norm.reshape(1, D_MODEL)

    kmem, vmem_ = _memkv_call(mem_prompt.reshape(batch * N_MEM, D_MODEL), row(mem_norm),
                              mem_w_k.astype(BF16), mem_w_v.astype(BF16))

    n_tiles = N_MEM * HEADS // SUBLANES
    kc = cache_mem_k.reshape(DEPTH, n_sample, n_tiles, SUBLANES, HEAD_DIM)
    vc = cache_mem_v.reshape(DEPTH, n_sample, n_tiles, SUBLANES, HEAD_DIM)

    stb = jnp.swapaxes(state_conv_b, 1, 2)
    stc = jnp.swapaxes(state_conv_c, 1, 2)

    xp = x_prompt.reshape(batch * seq, D_MODEL)
    xs = x_sample.reshape(n_sample, D_MODEL)
    cb_p, cb_s, cc_p, cc_s, gv_s = [], [], [], [], []
    ffn1_w = [a[0].astype(BF16) for a in ffn1_f32]
    for l in range(DEPTH):
        last = l == DEPTH - 1
        xp, xs, *mix_w = _ffn_call(xp, xs, l, ffn1_norm, *ffn1_w, None, mix_f32, l, name=f"ffn1_{l}")
        mw = dict(zip(mix_names, mix_w))
        xp, tb, tc, *ffn2_w = _mix_prompt_call(xp, kmem, vmem_, l, w, mw, ffn2_f32, batch=batch, seq=seq)
        xs, gv, nb, nc = _mix_sample_call(xs, kc, vc, stb, stc, l, w, mw)
        xp, xs, *ffn1_w = _ffn_call(xp, xs, l, ffn2_norm, *ffn2_w, final if last else None,
                                    [] if last else ffn1_f32, l + 1, name=f"ffn2_{l}")
        cb_p.append(tb); cc_p.append(tc); cb_s.append(nb); cc_s.append(nc); gv_s.append(gv)

    kv_shape = (DEPTH, batch, N_MEM, HEADS, HEAD_DIM)
    return (xp.reshape(batch, seq, D_MODEL), xs.reshape(n_sample, 1, D_MODEL),
            kmem.reshape(kv_shape), vmem_.reshape(kv_shape),
            jnp.stack(cb_p), jnp.swapaxes(jnp.stack(cb_s), 1, 2),
            jnp.stack(cc_p), jnp.swapaxes(jnp.stack(cc_s), 1, 2),
            jnp.stack(gv_s).reshape(DEPTH, n_sample, 1, HALF))
```

```python
import functools

import jax
import jax.numpy as jnp
from jax import lax
from jax.experimental import pallas as pl
from jax.experimental.pallas import tpu as pltpu

F32 = jnp.float32
BF16 = jnp.bfloat16

D_MODEL = 1024
DEPTH = 4
N_MEM = 256
CHUNK = 128
GROUPS = 4
HALF = 512
CONV_B_K = 31
CONV_C_K = 3
HEADS = 4
HEAD_DIM = 128
D_FF = 2816
EPS = 1e-6
ATTN_SCALE = HEAD_DIM ** -0.5

SUBLANES = 8
LANES = 128
BF16_TILE_ROWS = 16

MXU_TILE = 256
FF_CHUNKS = (1024, 1024, 768)
assert sum(FF_CHUNKS) == D_FF and all(c % MXU_TILE == 0 for c in FF_CHUNKS)
TM_FFN = 1024
TM_MIX = 512
CONV_ROWS = 64
HALO_B = 32
HALO_C = 8
SAMPLE_BLOCK = 8
VMEM_LIMIT = 56 * 1024 * 1024


def _dot(a, b):
    return jnp.dot(a, b, preferred_element_type=F32)


def _rms(x, g):
    y = x * lax.rsqrt(jnp.mean(x * x, axis=-1, keepdims=True) + EPS)
    return y * g


def _gelu(x):
    return 0.5 * x * (1.0 + lax.erf(x * (0.5 ** 0.5)))


def _layer_norm(x, g, b):
    mu = jnp.mean(x, axis=-1, keepdims=True)
    xc = x - mu
    var = jnp.mean(xc * xc, axis=-1, keepdims=True)
    return xc * lax.rsqrt(var + EPS) * g + b


def _cast_specs(arrays, layer, n_steps, step_of):
    in_specs, out_specs, out_shapes = [], [], []
    for a in arrays:
        rows, cols = a.shape[1:]
        n_blocks = n_steps
        while rows % (n_blocks * BF16_TILE_ROWS):
            n_blocks //= 2
        per = n_steps // n_blocks
        rb = rows // n_blocks
        in_specs.append(pl.BlockSpec((None, rb, cols), lambda *g, per=per: (layer, step_of(*g) // per, 0)))
        out_specs.append(pl.BlockSpec((rb, cols), lambda *g, per=per: (step_of(*g) // per, 0)))
        out_shapes.append(jax.ShapeDtypeStruct((rows, cols), BF16))
    return in_specs, out_specs, out_shapes


def _cast_blocks(in_refs, out_refs):
    for i_ref, o_ref in zip(in_refs, out_refs):
        o_ref[...] = i_ref[...].astype(BF16)


def _ffn_kernel(*refs, final, n_cast):
    n_in = 5 + int(final)
    xp_ref, xs_ref, g_ref, wgu_ref, wd_ref = refs[:5]
    fg_ref = refs[5] if final else None
    cast_in = refs[n_in:n_in + n_cast]
    op_ref, os_ref = refs[n_in + n_cast:n_in + n_cast + 2]
    cast_out = refs[n_in + n_cast + 2:]
    _cast_blocks(cast_in, cast_out)

    def rows(x_ref, o_ref):
        x = x_ref[...]
        h = _rms(x, g_ref[...]).astype(BF16)
        acc = None
        lo = 0
        for width in FF_CHUNKS:
            gate = _dot(h, wgu_ref[:, lo:lo + width])
            up = _dot(h, wgu_ref[:, D_FF + lo:D_FF + lo + width])
            a = (jax.nn.silu(gate) * up).astype(BF16)
            d = _dot(a, wd_ref[lo:lo + width, :])
            acc = d if acc is None else acc + d
            lo += width
        y = x + 0.5 * acc
        if final:
            y = _rms(y, fg_ref[...])
        o_ref[...] = y

    rows(xp_ref, op_ref)

    @pl.when(pl.program_id(0) == pl.num_programs(0) - 1)
    def _():
        rows(xs_ref, os_ref)


def _resident(shape, layer=None):
    nd = len(shape)
    if layer is None:
        return pl.BlockSpec(tuple(shape), lambda *_: (0,) * nd, pipeline_mode=pl.Buffered(1))
    return pl.BlockSpec((None,) + tuple(shape), lambda *_: (layer,) + (0,) * nd,
                        pipeline_mode=pl.Buffered(1))


def _ffn_call(xp, xs, layer, norm, wgu, wd, final_norm, cast, cast_layer, *, name):
    m, n = xp.shape[0], xs.shape[0]
    tm = TM_FFN
    n_steps = m // tm
    final = final_norm is not None
    sample_spec = pl.BlockSpec((n, D_MODEL), lambda i: (0, 0))
    in_specs = [
        pl.BlockSpec((tm, D_MODEL), lambda i: (i, 0)),
        sample_spec,
        _resident((1, D_MODEL), layer),
        _resident((D_MODEL, 2 * D_FF)),
        _resident((D_FF, D_MODEL)),
    ]
    args = [xp, xs, norm, wgu, wd]
    if final:
        in_specs.append(pl.BlockSpec((1, D_MODEL), lambda i: (0, 0)))
        args.append(final_norm)
    cast_in, cast_out, cast_shapes = _cast_specs(cast, cast_layer, n_steps, lambda i: i)
    return pl.pallas_call(
        functools.partial(_ffn_kernel, final=final, n_cast=len(cast)),
        grid=(n_steps,),
        in_specs=in_specs + cast_in,
        out_specs=[pl.BlockSpec((tm, D_MODEL), lambda i: (i, 0)), sample_spec] + cast_out,
        out_shape=[jax.ShapeDtypeStruct((m, D_MODEL), F32),
                   jax.ShapeDtypeStruct((n, D_MODEL), F32)] + cast_shapes,
        compiler_params=pltpu.CompilerParams(
            dimension_semantics=("arbitrary",), vmem_limit_bytes=VMEM_LIMIT),
        name=name,
    )(*args, *cast)


def _memkv_kernel(m_ref, g_ref, wk_ref, wv_ref, k_ref, v_ref):
    m = _rms(m_ref[...], g_ref[...]).astype(BF16)
    k = _dot(m, wk_ref[...])
    v = _dot(m, wv_ref[...])
    for hd in range(HEADS):
        k_ref[pl.ds(hd, N_MEM, stride=HEADS), :] = k[:, hd * HEAD_DIM:(hd + 1) * HEAD_DIM]
        v_ref[pl.ds(hd, N_MEM, stride=HEADS), :] = v[:, hd * HEAD_DIM:(hd + 1) * HEAD_DIM]


def _memkv_call(mem, norm, wk, wv):
    batch = mem.shape[0] // N_MEM
    tm = N_MEM
    per_layer = lambda shape: pl.BlockSpec((None,) + shape, lambda l, i: (l, 0, 0))
    out_spec = pl.BlockSpec((None, None, N_MEM * HEADS, HEAD_DIM), lambda l, i: (l, i, 0, 0))
    return pl.pallas_call(
        _memkv_kernel,
        grid=(DEPTH, batch),
        in_specs=[pl.BlockSpec((tm, D_MODEL), lambda l, i: (i, 0)),
                  per_layer((1, D_MODEL)), per_layer((D_MODEL, HALF)), per_layer((D_MODEL, HALF))],
        out_specs=[out_spec, out_spec],
        out_shape=[jax.ShapeDtypeStruct((DEPTH, batch, N_MEM * HEADS, HEAD_DIM), F32)] * 2,
        compiler_params=pltpu.CompilerParams(
            dimension_semantics=("arbitrary", "arbitrary"), vmem_limit_bytes=VMEM_LIMIT),
        name="memkv",
    )(mem, norm, wk, wv)


def _causal_conv(ext_ref, w_ref, taps, first_row, rows):
    outs = []
    for r in range(rows // CONV_ROWS):
        base = first_row + r * CONV_ROWS
        acc = None
        for k in range(taps):
            term = w_ref[k:k + 1, :] * ext_ref[base + k:base + k + CONV_ROWS, :]
            acc = term if acc is None else acc + term
        outs.append(acc)
    return jnp.concatenate(outs, axis=0)


def _shifted_copies(ext_ref, shifted_ref):
    total = ext_ref.shape[0]
    tiles = ext_ref[...].reshape(total // SUBLANES, SUBLANES, HALF)
    sub = lax.broadcasted_iota(jnp.int32, (1, SUBLANES, HALF), 1)
    for s in range(1, SUBLANES):
        rot = pltpu.roll(tiles, SUBLANES - s, axis=1)
        nxt = jnp.concatenate([rot[1:], rot[:1]], axis=0)
        shifted_ref[s - 1] = jnp.where(sub < SUBLANES - s, rot, nxt).reshape(total, HALF)


def _conv_chunks(ext_ref, shifted_ref, w_ref, out_ref, taps, first_row, chunks):
    for r in chunks:
        acc = None
        for k in range(taps):
            s = (first_row + k) % SUBLANES
            base = r * CONV_ROWS + first_row + k - s
            src = ext_ref if s == 0 else shifted_ref.at[s - 1]
            win = src[base:base + CONV_ROWS, :].reshape(CONV_ROWS // SUBLANES, SUBLANES, HALF)
            term = w_ref[k][None] * win
            acc = term if acc is None else acc + term
        out_ref[r * CONV_ROWS:(r + 1) * CONV_ROWS, :] = acc.reshape(CONV_ROWS, HALF)


def _mix_prompt_kernel(x_ref, k_ref, v_ref, norm_ref, win_ref, lng_ref, lnb_ref, ws_ref, bst_ref,
                       wa_ref, cbw_ref, cbb_ref, cblg_ref, cblb_ref, wb_ref, ccw_ref, wc_ref, wm_ref,
                       wg_ref, bg_ref, wo_ref, cast_gu_ref, cast_d_ref,
                       xo_ref, tailb_ref, tailc_ref, wgu_out_ref, wd_out_ref,
                       extb_ref, extc_ref, shiftb_ref, h_scr, yb_scr, merged_scr):
    _cast_blocks((cast_gu_ref, cast_d_ref), (wgu_out_ref, wd_out_ref))
    tm = x_ref.shape[0]
    j = pl.program_id(1)
    n_conv = tm // CONV_ROWS
    conv_split = (0, 6 * n_conv // 16, 11 * n_conv // 16, n_conv)

    def gate(i):
        z = _dot(h_scr[...], wg_ref[:, i * D_MODEL:(i + 1) * D_MODEL])
        return jax.nn.sigmoid(z + bg_ref[:, i * D_MODEL:(i + 1) * D_MODEL])

    def conv_b(part):
        _conv_chunks(extb_ref, shiftb_ref, cbw_ref, yb_scr, CONV_B_K, HALO_B - (CONV_B_K - 1),
                     range(conv_split[part], conv_split[part + 1]))

    @pl.when(j == 0)
    def _():
        extb_ref[0:HALO_B, :] = jnp.zeros((HALO_B, HALF), F32)
        extc_ref[0:HALO_C, :] = jnp.zeros((HALO_C, HALF), F32)

    h = _rms(x_ref[...], norm_ref[...]).astype(BF16)
    h_scr[...] = h

    zb = _dot(h, win_ref[:, 2 * HALF:4 * HALF])
    extb_ref[HALO_B:HALO_B + tm, :] = zb[:, :HALF] * jax.nn.sigmoid(zb[:, HALF:])
    _shifted_copies(extb_ref, shiftb_ref)

    conv_b(0)
    za = _gelu(_dot(h_scr[...], win_ref[:, 0:2 * HALF]))
    u = za[:, :HALF]
    v = _layer_norm(za[:, HALF:], lng_ref[...], lnb_ref[...]).astype(BF16)
    n_chunks = tm // CHUNK
    row = lax.broadcasted_iota(jnp.int32, (CHUNK, CHUNK), 0)
    col = lax.broadcasted_iota(jnp.int32, (CHUNK, CHUNK), 1)
    pieces = [[None] * GROUPS for _ in range(n_chunks)]
    for g in range(GROUPS):
        ws = jnp.where(row >= col, ws_ref[g], 0.0).astype(BF16)
        rhs = jnp.concatenate(
            [v[c * CHUNK:(c + 1) * CHUNK, g * LANES:(g + 1) * LANES] for c in range(n_chunks)], axis=1)
        res = _dot(ws, rhs) + bst_ref[:, g:g + 1]
        for c in range(n_chunks):
            pieces[c][g] = res[:, c * CHUNK:(c + 1) * CHUNK]
    mixed = jnp.concatenate([jnp.concatenate(p, axis=1) for p in pieces], axis=0)
    merged_scr[...] = gate(0) * _dot((u * mixed).astype(BF16), wa_ref[...])

    conv_b(1)
    zc = _dot(h_scr[...], win_ref[:, 4 * HALF:7 * HALF])
    extc_ref[HALO_C:HALO_C + tm, :] = zc[:, HALF:2 * HALF] * zc[:, 2 * HALF:]
    yc = _causal_conv(extc_ref, ccw_ref, CONV_C_K, HALO_C - (CONV_C_K - 1), tm)
    merged_scr[...] += gate(2) * _dot((zc[:, :HALF] * yc).astype(BF16), wc_ref[...])

    conv_b(2)
    q = _dot(h_scr[...], win_ref[:, 7 * HALF:8 * HALF])
    ones = jnp.ones((N_MEM, HEAD_DIM), BF16)
    heads = []
    for hd in range(HEADS):
        sl = slice(hd * HEAD_DIM, (hd + 1) * HEAD_DIM)
        kh = k_ref[pl.ds(hd, N_MEM, stride=HEADS), :].astype(BF16)
        vh = v_ref[pl.ds(hd, N_MEM, stride=HEADS), :].astype(BF16)
        s = lax.dot_general(q[:, sl].astype(BF16), kh, (((1,), (1,)), ((), ())),
                            preferred_element_type=F32) * ATTN_SCALE
        e = jnp.exp(s - jnp.max(s, axis=-1, keepdims=True)).astype(BF16)
        r = _dot(e, jnp.concatenate([vh, ones], axis=1))
        heads.append(r[:, :HEAD_DIM] / r[:, HEAD_DIM:])
    om = jnp.concatenate(heads, axis=1).astype(BF16)
    merged_scr[...] += gate(3) * _dot(om, wm_ref[...])

    yb = yb_scr[...] + cbb_ref[...]
    yb = jax.nn.silu(_layer_norm(yb, cblg_ref[...], cblb_ref[...]))
    merged = merged_scr[...] + gate(1) * _dot(yb.astype(BF16), wb_ref[...])
    xo_ref[...] = x_ref[...] + _dot(merged.astype(BF16), wo_ref[...])

    @pl.when(j == pl.num_programs(1) - 1)
    def _():
        tailb_ref[...] = extb_ref[tm + HALO_B - (CONV_B_K - 1):tm + HALO_B, :]
        tailc_ref[...] = extc_ref[tm + HALO_C - (CONV_C_K - 1):tm + HALO_C, :]

    extb_ref[0:HALO_B, :] = extb_ref[tm:tm + HALO_B, :]
    extc_ref[0:HALO_C, :] = extc_ref[tm:tm + HALO_C, :]


def _mix_prompt_call(x, kmem, vmem_, layer, w, mw, cast, *, batch, seq):
    tm = TM_MIX
    nj = seq // tm
    cast_in, cast_out, cast_shapes = _cast_specs(cast, layer, batch * nj, lambda b, j: b * nj + j)
    in_specs = [
        pl.BlockSpec((tm, D_MODEL), lambda b, j: (b * nj + j, 0)),
        pl.BlockSpec((None, None, N_MEM * HEADS, HEAD_DIM), lambda b, j: (layer, b, 0, 0)),
        pl.BlockSpec((None, None, N_MEM * HEADS, HEAD_DIM), lambda b, j: (layer, b, 0, 0)),
        _resident((1, D_MODEL), layer),
        _resident((D_MODEL, 8 * HALF)),
        _resident((1, HALF), layer),
        _resident((1, HALF), layer),
        _resident((GROUPS, CHUNK, CHUNK), layer),
        _resident((CHUNK, GROUPS), layer),
        _resident((HALF, D_MODEL)),
        _resident((CONV_B_K, SUBLANES, HALF), layer),
        _resident((1, HALF), layer),
        _resident((1, HALF), layer),
        _resident((1, HALF), layer),
        _resident((HALF, D_MODEL)),
        _resident((CONV_C_K, HALF), layer),
        _resident((HALF, D_MODEL)),
        _resident((HALF, D_MODEL)),
        _resident((D_MODEL, 4 * D_MODEL)),
        _resident((1, 4 * D_MODEL), layer),
        _resident((D_MODEL, D_MODEL)),
    ]
    out_specs = [
        pl.BlockSpec((tm, D_MODEL), lambda b, j: (b * nj + j, 0)),
        pl.BlockSpec((None, CONV_B_K - 1, HALF), lambda b, j: (b, 0, 0)),
        pl.BlockSpec((None, CONV_C_K - 1, HALF), lambda b, j: (b, 0, 0)),
    ]
    out_shape = [
        jax.ShapeDtypeStruct((batch * seq, D_MODEL), F32),
        jax.ShapeDtypeStruct((batch, CONV_B_K - 1, HALF), F32),
        jax.ShapeDtypeStruct((batch, CONV_C_K - 1, HALF), F32),
    ]
    return pl.pallas_call(
        _mix_prompt_kernel,
        grid=(batch, nj),
        in_specs=in_specs + cast_in,
        out_specs=out_specs + cast_out,
        out_shape=out_shape + cast_shapes,
        scratch_shapes=[pltpu.VMEM((HALO_B + tm, HALF), F32), pltpu.VMEM((HALO_C + tm, HALF), F32),
                        pltpu.VMEM((SUBLANES - 1, HALO_B + tm, HALF), F32),
                        pltpu.VMEM((tm, D_MODEL), BF16),
                        pltpu.VMEM((tm, HALF), F32),
                        pltpu.VMEM((tm, D_MODEL), F32)],
        compiler_params=pltpu.CompilerParams(
            dimension_semantics=("arbitrary", "arbitrary"), vmem_limit_bytes=VMEM_LIMIT),
        name=f"mix_prompt_{layer}",
    )(x, kmem, vmem_, w["mix_norm"], mw["w_in"], w["gmlp_ln_g"], w["gmlp_ln_b"], w["gmlp_w_s"],
      w["gmlp_b_s_t"], mw["gmlp_w_out"], w["conv_b_w8"], w["conv_b_bias"], w["conv_b_ln_g"],
      w["conv_b_ln_b"], mw["conv_b_w_out"], w["conv_c_w"], mw["conv_c_w_out"], mw["mem_w_out"],
      mw["w_branch_gate"], w["b_branch_gate"], mw["w_o"], *cast)


def _mix_sample_kernel(x_ref, k_ref, v_ref, stb_ref, stc_ref, norm_ref, win_ref, lng_ref, lnb_ref,
                       ws0_ref, bs0_ref, wa_ref, cbw_ref, cbb_ref, cblg_ref, cblb_ref, wb_ref, ccw_ref,
                       wc_ref, wm_ref, wg_ref, bg_ref, wo_ref,
                       xo_ref, gv_ref, nstb_ref, nstc_ref,
                       h_scr, pa_scr, xinb_scr, gatec_scr, xinc_scr, q_scr, yb_scr, yc_scr, o_scr):
    i = pl.program_id(0)
    sb = stb_ref.shape[1]
    r0 = pl.multiple_of(i * sb, sb)

    @pl.when(i == 0)
    def _():
        h = _rms(x_ref[...], norm_ref[...]).astype(BF16)
        h_scr[...] = h
        za = _gelu(_dot(h, win_ref[:, 0:2 * HALF]))
        v = _layer_norm(za[:, HALF:], lng_ref[...], lnb_ref[...])
        gv_ref[...] = v
        pa_scr[...] = za[:, :HALF] * (ws0_ref[...] * v + bs0_ref[...])
        zb = _dot(h, win_ref[:, 2 * HALF:4 * HALF])
        xinb_scr[...] = zb[:, :HALF] * jax.nn.sigmoid(zb[:, HALF:])
        zc = _dot(h, win_ref[:, 4 * HALF:7 * HALF])
        gatec_scr[...] = zc[:, :HALF]
        xinc_scr[...] = zc[:, HALF:2 * HALF] * zc[:, 2 * HALF:]
        q = _dot(h, win_ref[:, 7 * HALF:8 * HALF])
        for hd in range(HEADS):
            qh = q[:, hd * HEAD_DIM:(hd + 1) * HEAD_DIM]
            q_scr[:, hd, :] = qh
            q_scr[:, hd + HEADS, :] = qh

    xin_b = xinb_scr[pl.ds(r0, sb), :]
    acc = cbw_ref[CONV_B_K - 1:CONV_B_K, :] * xin_b
    for k in range(CONV_B_K - 1):
        acc = acc + cbw_ref[k:k + 1, :] * stb_ref[k]
    yb_scr[pl.ds(r0, sb), :] = acc
    nstb_ref[0:CONV_B_K - 2] = stb_ref[1:CONV_B_K - 1]
    nstb_ref[CONV_B_K - 2] = xin_b

    xin_c = xinc_scr[pl.ds(r0, sb), :]
    yc_scr[pl.ds(r0, sb), :] = (ccw_ref[0:1, :] * stc_ref[0] + ccw_ref[1:2, :] * stc_ref[1]
                                + ccw_ref[2:3, :] * xin_c)
    nstc_ref[0] = stc_ref[1]
    nstc_ref[1] = xin_c

    ones = jnp.ones((HEAD_DIM, LANES), BF16)
    n_tiles = N_MEM * HEADS // SUBLANES
    for s in range(sb):
        q8 = q_scr[r0 + s]
        prod = (k_ref[s] * q8[None]).reshape(n_tiles * SUBLANES, HEAD_DIM)
        sc = (_dot(prod.astype(BF16), ones) * ATTN_SCALE).reshape(n_tiles, SUBLANES, LANES)
        mx = jnp.max(sc, axis=0)
        mx = jnp.maximum(mx, pltpu.roll(mx, HEADS, axis=0))
        e = jnp.exp(sc - mx[None])
        num = jnp.sum(e * v_ref[s], axis=0)
        den = jnp.sum(e, axis=0)
        num = num + pltpu.roll(num, HEADS, axis=0)
        den = den + pltpu.roll(den, HEADS, axis=0)
        o_scr[r0 + s] = num / den

    @pl.when(i == pl.num_programs(0) - 1)
    def _():
        h = h_scr[...]

        def gate(n):
            z = _dot(h, wg_ref[:, n * D_MODEL:(n + 1) * D_MODEL]) + bg_ref[:, n * D_MODEL:(n + 1) * D_MODEL]
            return jax.nn.sigmoid(z)

        merged = gate(0) * _dot(pa_scr[...].astype(BF16), wa_ref[...])
        yb = jax.nn.silu(_layer_norm(yb_scr[...] + cbb_ref[...], cblg_ref[...], cblb_ref[...]))
        merged = merged + gate(1) * _dot(yb.astype(BF16), wb_ref[...])
        merged = merged + gate(2) * _dot((gatec_scr[...] * yc_scr[...]).astype(BF16), wc_ref[...])
        om = None
        for hd in range(HEADS):
            part = _dot(o_scr[:, hd, :].astype(BF16), wm_ref[hd * HEAD_DIM:(hd + 1) * HEAD_DIM, :])
            om = part if om is None else om + part
        merged = merged + gate(3) * om
        xo_ref[...] = x_ref[...] + _dot(merged.astype(BF16), wo_ref[...])


def _mix_sample_call(x, kc, vc, stb, stc, layer, w, mw):
    n = x.shape[0]
    sb = SAMPLE_BLOCK
    n_tiles = N_MEM * HEADS // SUBLANES
    const = lambda shape: pl.BlockSpec(shape, lambda i: (0,) * len(shape))
    in_specs = [
        const((n, D_MODEL)),
        pl.BlockSpec((None, sb, n_tiles, SUBLANES, HEAD_DIM), lambda i: (layer, i, 0, 0, 0)),
        pl.BlockSpec((None, sb, n_tiles, SUBLANES, HEAD_DIM), lambda i: (layer, i, 0, 0, 0)),
        pl.BlockSpec((None, CONV_B_K - 1, sb, HALF), lambda i: (layer, 0, i, 0)),
        pl.BlockSpec((None, CONV_C_K - 1, sb, HALF), lambda i: (layer, 0, i, 0)),
        _resident((1, D_MODEL), layer),
        _resident((D_MODEL, 8 * HALF)),
        _resident((1, HALF), layer),
        _resident((1, HALF), layer),
        _resident((1, HALF), layer),
        _resident((1, HALF), layer),
        _resident((HALF, D_MODEL)),
        _resident((CONV_B_K, HALF), layer),
        _resident((1, HALF), layer),
        _resident((1, HALF), layer),
        _resident((1, HALF), layer),
        _resident((HALF, D_MODEL)),
        _resident((CONV_C_K, HALF), layer),
        _resident((HALF, D_MODEL)),
        _resident((HALF, D_MODEL)),
        _resident((D_MODEL, 4 * D_MODEL)),
        _resident((1, 4 * D_MODEL), layer),
        _resident((D_MODEL, D_MODEL)),
    ]
    out_specs = [
        const((n, D_MODEL)),
        const((n, HALF)),
        pl.BlockSpec((CONV_B_K - 1, sb, HALF), lambda i: (0, i, 0)),
        pl.BlockSpec((CONV_C_K - 1, sb, HALF), lambda i: (0, i, 0)),
    ]
    out_shape = [
        jax.ShapeDtypeStruct((n, D_MODEL), F32),
        jax.ShapeDtypeStruct((n, HALF), F32),
        jax.ShapeDtypeStruct((CONV_B_K - 1, n, HALF), F32),
        jax.ShapeDtypeStruct((CONV_C_K - 1, n, HALF), F32),
    ]
    scratch = [
        pltpu.VMEM((n, D_MODEL), BF16),
        pltpu.VMEM((n, HALF), F32),
        pltpu.VMEM((n, HALF), F32),
        pltpu.VMEM((n, HALF), F32),
        pltpu.VMEM((n, HALF), F32),
        pltpu.VMEM((n, SUBLANES, HEAD_DIM), F32),
        pltpu.VMEM((n, HALF), F32),
        pltpu.VMEM((n, HALF), F32),
        pltpu.VMEM((n, SUBLANES, HEAD_DIM), F32),
    ]
    return pl.pallas_call(
        _mix_sample_kernel,
        grid=(n // sb,),
        in_specs=in_specs,
        out_specs=out_specs,
        out_shape=out_shape,
        scratch_shapes=scratch,
        compiler_params=pltpu.CompilerParams(
            dimension_semantics=("arbitrary",), vmem_limit_bytes=VMEM_LIMIT),
        name=f"mix_sample_{layer}",
    )(x, kc, vc, stb, stc, w["mix_norm"], mw["w_in"], w["gmlp_ln_g"], w["gmlp_ln_b"], w["gmlp_ws0"],
      w["gmlp_bs0"], mw["gmlp_w_out"], w["conv_b_w"], w["conv_b_bias"], w["conv_b_ln_g"],
      w["conv_b_ln_b"], mw["conv_b_w_out"], w["conv_c_w"], mw["conv_c_w_out"], mw["mem_w_out"],
      mw["w_branch_gate"], w["b_branch_gate"], mw["w_o"])


def kernel(x_prompt, x_sample, mem_prompt, state_conv_b, state_conv_c, cache_mem_k, cache_mem_v,
           ffn1_norm, ffn1_w_gate_up, ffn1_w_down, mix_norm, w_in,
           gmlp_ln_g, gmlp_ln_b, gmlp_w_s, gmlp_b_s, gmlp_w_out,
           conv_b_w, conv_b_bias, conv_b_ln_g, conv_b_ln_b, conv_b_w_out,
           conv_c_w, conv_c_w_out, mem_norm, mem_w_k, mem_w_v, mem_w_out,
           w_branch_gate, b_branch_gate, w_o, ffn2_norm, ffn2_w_gate_up, ffn2_w_down, final_norm):
    batch, seq, _ = x_prompt.shape
    n_sample = x_sample.shape[0]
    row = lambda a: a.reshape(DEPTH, 1, a.shape[-1])

    w = {
        "mix_norm": row(mix_norm),
        "gmlp_ln_g": row(gmlp_ln_g), "gmlp_ln_b": row(gmlp_ln_b),
        "gmlp_w_s": gmlp_w_s, "gmlp_b_s_t": jnp.swapaxes(gmlp_b_s, 1, 2),
        "gmlp_ws0": row(jnp.repeat(gmlp_w_s[:, :, 0, 0], LANES, axis=1)),
        "gmlp_bs0": row(jnp.repeat(gmlp_b_s[:, :, 0], LANES, axis=1)),
        "conv_b_w": conv_b_w, "conv_b_bias": row(conv_b_bias),
        "conv_b_w8": jnp.broadcast_to(conv_b_w[:, :, None, :], (DEPTH, CONV_B_K, SUBLANES, HALF)),
        "conv_b_ln_g": row(conv_b_ln_g), "conv_b_ln_b": row(conv_b_ln_b),
        "conv_c_w": conv_c_w,
        "b_branch_gate": row(b_branch_gate),
    }
    mix_names = ("w_in", "w_branch_gate", "w_o", "gmlp_w_out", "conv_b_w_out", "conv_c_w_out", "mem_w_out")
    mix_f32 = [w_in, w_branch_gate, w_o, gmlp_w_out, conv_b_w_out, conv_c_w_out, mem_w_out]
    ffn1_f32 = [ffn1_w_gate_up, ffn1_w_down]
    ffn2_f32 = [ffn2_w_gate_up, ffn2_w_down]
    ffn1_norm, ffn2_norm = row(ffn1_norm), row(ffn2_norm)
    final = final_norm.reshape(1, D_MODEL)

    kmem, vmem_ = _memkv_call(mem_prompt.reshape(batch * N_MEM, D_MODEL), row(mem_norm),
                              mem_w_k.astype(BF16), mem_w_v.astype(BF16))

    n_tiles = N_MEM * HEADS // SUBLANES
    kc = cache_mem_k.reshape(DEPTH, n_sample, n_tiles, SUBLANES, HEAD_DIM)
    vc = cache_mem_v.reshape(DEPTH, n_sample, n_tiles, SUBLANES, HEAD_DIM)

    stb = jnp.swapaxes(state_conv_b, 1, 2)
    stc = jnp.swapaxes(state_conv_c, 1, 2)

    xp = x_prompt.reshape(batch * seq, D_MODEL)
    xs = x_sample.reshape(n_sample, D_MODEL)
    cb_p, cb_s, cc_p, cc_s, gv_s = [], [], [], [], []
    ffn1_w = [a[0].astype(BF16) for a in ffn1_f32]
    for l in range(DEPTH):
        last = l == DEPTH - 1
        xp, xs, *mix_w = _ffn_call(xp, xs, l, ffn1_norm, *ffn1_w, None, mix_f32, l, name=f"ffn1_{l}")
        mw = dict(zip(mix_names, mix_w))
        xp, tb, tc, *ffn2_w = _mix_prompt_call(xp, kmem, vmem_, l, w, mw, ffn2_f32, batch=batch, seq=seq)
        xs, gv, nb, nc = _mix_sample_call(xs, kc, vc, stb, stc, l, w, mw)
        xp, xs, *ffn1_w = _ffn_call(xp, xs, l, ffn2_norm, *ffn2_w, final if last else None,
                                    [] if last else ffn1_f32, l + 1, name=f"ffn2_{l}")
        cb_p.append(tb); cc_p.append(tc); cb_s.append(nb); cc_s.append(nc); gv_s.append(gv)

    kv_shape = (DEPTH, batch, N_MEM, HEADS, HEAD_DIM)
    return (xp.reshape(batch, seq, D_MODEL), xs.reshape(n_sample, 1, D_MODEL),
            kmem.reshape(kv_shape), vmem_.reshape(kv_shape),
            jnp.stack(cb_p), jnp.swapaxes(jnp.stack(cb_s), 1, 2),
            jnp.stack(cc_p), jnp.swapaxes(jnp.stack(cc_s), 1, 2),
            jnp.stack(gv_s).reshape(DEPTH, n_sample, 1, HALF))
```

```python
import functools

import jax
import jax.numpy as jnp
from jax import lax
from jax.experimental import pallas as pl
from jax.experimental.pallas import tpu as pltpu

F32 = jnp.float32
BF16 = jnp.bfloat16

D_MODEL = 1024
DEPTH = 4
N_MEM = 256
CHUNK = 128
GROUPS = 4
HALF = 512
CONV_B_K = 31
CONV_C_K = 3
HEADS = 4
HEAD_DIM = 128
D_FF = 2816
EPS = 1e-6
ATTN_SCALE = HEAD_DIM ** -0.5

SUBLANES = 8
LANES = 128
BF16_TILE_ROWS = 16

MXU_TILE = 256
FF_CHUNKS = (1024, 1024, 768)
assert sum(FF_CHUNKS) == D_FF and all(c % MXU_TILE == 0 for c in FF_CHUNKS)
TM_FFN = 1024
TM_MIX = 512
CONV_ROWS = 64
HALO_B = 32
HALO_C = 8
SAMPLE_BLOCK = 8
VMEM_LIMIT = 56 * 1024 * 1024


def _dot(a, b):
    return jnp.dot(a, b, preferred_element_type=F32)


def _rms(x, g):
    y = x * lax.rsqrt(jnp.mean(x * x, axis=-1, keepdims=True) + EPS)
    return y * g


def _gelu(x):
    return 0.5 * x * (1.0 + lax.erf(x * (0.5 ** 0.5)))


def _layer_norm(x, g, b):
    mu = jnp.mean(x, axis=-1, keepdims=True)
    xc = x - mu
    var = jnp.mean(xc * xc, axis=-1, keepdims=True)
    return xc * lax.rsqrt(var + EPS) * g + b


def _cast_specs(arrays, layer, n_steps, step_of):
    in_specs, out_specs, out_shapes = [], [], []
    for a in arrays:
        rows, cols = a.shape[1:]
        n_blocks = n_steps
        while rows % (n_blocks * BF16_TILE_ROWS):
            n_blocks //= 2
        per = n_steps // n_blocks
        rb = rows // n_blocks
        in_specs.append(pl.BlockSpec((None, rb, cols), lambda *g, per=per: (layer, step_of(*g) // per, 0)))
        out_specs.append(pl.BlockSpec((rb, cols), lambda *g, per=per: (step_of(*g) // per, 0)))
        out_shapes.append(jax.ShapeDtypeStruct((rows, cols), BF16))
    return in_specs, out_specs, out_shapes


def _cast_blocks(in_refs, out_refs):
    for i_ref, o_ref in zip(in_refs, out_refs):
        o_ref[...] = i_ref[...].astype(BF16)


def _ffn_kernel(*refs, final, n_cast):
    n_in = 5 + int(final)
    xp_ref, xs_ref, g_ref, wgu_ref, wd_ref = refs[:5]
    fg_ref = refs[5] if final else None
    cast_in = refs[n_in:n_in + n_cast]
    op_ref, os_ref = refs[n_in + n_cast:n_in + n_cast + 2]
    cast_out = refs[n_in + n_cast + 2:]
    _cast_blocks(cast_in, cast_out)

    def rows(x_ref, o_ref):
        x = x_ref[...]
        h = _rms(x, g_ref[...]).astype(BF16)
        acc = None
        lo = 0
        for width in FF_CHUNKS:
            gate = _dot(h, wgu_ref[:, lo:lo + width])
            up = _dot(h, wgu_ref[:, D_FF + lo:D_FF + lo + width])
            a = (jax.nn.silu(gate) * up).astype(BF16)
            d = _dot(a, wd_ref[lo:lo + width, :])
            acc = d if acc is None else acc + d
            lo += width
        y = x + 0.5 * acc
        if final:
            y = _rms(y, fg_ref[...])
        o_ref[...] = y

    rows(xp_ref, op_ref)

    @pl.when(pl.program_id(0) == pl.num_programs(0) - 1)
    def _():
        rows(xs_ref, os_ref)


def _resident(shape, layer=None):
    nd = len(shape)
    if layer is None:
        return pl.BlockSpec(tuple(shape), lambda *_: (0,) * nd, pipeline_mode=pl.Buffered(1))
    return pl.BlockSpec((None,) + tuple(shape), lambda *_: (layer,) + (0,) * nd,
                        pipeline_mode=pl.Buffered(1))


def _ffn_call(xp, xs, layer, norm, wgu, wd, final_norm, cast, cast_layer, *, name):
    m, n = xp.shape[0], xs.shape[0]
    tm = TM_FFN
    n_steps = m // tm
    final = final_norm is not None
    sample_spec = pl.BlockSpec((n, D_MODEL), lambda i: (0, 0))
    in_specs = [
        pl.BlockSpec((tm, D_MODEL), lambda i: (i, 0)),
        sample_spec,
        _resident((1, D_MODEL), layer),
        _resident((D_MODEL, 2 * D_FF)),
        _resident((D_FF, D_MODEL)),
    ]
    args = [xp, xs, norm, wgu, wd]
    if final:
        in_specs.append(pl.BlockSpec((1, D_MODEL), lambda i: (0, 0)))
        args.append(final_norm)
    cast_in, cast_out, cast_shapes = _cast_specs(cast, cast_layer, n_steps, lambda i: i)
    return pl.pallas_call(
        functools.partial(_ffn_kernel, final=final, n_cast=len(cast)),
        grid=(n_steps,),
        in_specs=in_specs + cast_in,
        out_specs=[pl.BlockSpec((tm, D_MODEL), lambda i: (i, 0)), sample_spec] + cast_out,
        out_shape=[jax.ShapeDtypeStruct((m, D_MODEL), F32),
                   jax.ShapeDtypeStruct((n, D_MODEL), F32)] + cast_shapes,
        compiler_params=pltpu.CompilerParams(
            dimension_semantics=("arbitrary",), vmem_limit_bytes=VMEM_LIMIT),
        name=name,
    )(*args, *cast)


def _memkv_kernel(m_ref, g_ref, wk_ref, wv_ref, k_ref, v_ref):
    x = m_ref[...]
    xn = x * lax.rsqrt(jnp.mean(x * x, axis=-1, keepdims=True) + EPS)
    for l in range(DEPTH):
        m = (xn * g_ref[l]).astype(BF16)
        k = _dot(m, wk_ref[l])
        v = _dot(m, wv_ref[l])
        for hd in range(HEADS):
            k_ref[l, pl.ds(hd, N_MEM, stride=HEADS), :] = k[:, hd * HEAD_DIM:(hd + 1) * HEAD_DIM]
            v_ref[l, pl.ds(hd, N_MEM, stride=HEADS), :] = v[:, hd * HEAD_DIM:(hd + 1) * HEAD_DIM]


def _memkv_call(mem, norm, wk, wv):
    batch = mem.shape[0] // N_MEM
    out_spec = pl.BlockSpec((DEPTH, None, N_MEM * HEADS, HEAD_DIM), lambda i: (0, i, 0, 0))
    return pl.pallas_call(
        _memkv_kernel,
        grid=(batch,),
        in_specs=[pl.BlockSpec((N_MEM, D_MODEL), lambda i: (i, 0)),
                  _resident((DEPTH, 1, D_MODEL)), _resident((DEPTH, D_MODEL, HALF)),
                  _resident((DEPTH, D_MODEL, HALF))],
        out_specs=[out_spec, out_spec],
        out_shape=[jax.ShapeDtypeStruct((DEPTH, batch, N_MEM * HEADS, HEAD_DIM), F32)] * 2,
        compiler_params=pltpu.CompilerParams(
            dimension_semantics=("arbitrary",), vmem_limit_bytes=VMEM_LIMIT),
        name="memkv",
    )(mem, norm, wk, wv)


def _causal_conv(ext_ref, w_ref, taps, first_row, rows):
    outs = []
    for r in range(rows // CONV_ROWS):
        base = first_row + r * CONV_ROWS
        acc = None
        for k in range(taps):
            term = w_ref[k:k + 1, :] * ext_ref[base + k:base + k + CONV_ROWS, :]
            acc = term if acc is None else acc + term
        outs.append(acc)
    return jnp.concatenate(outs, axis=0)


def _shifted_copies(ext_ref, shifted_ref):
    total = ext_ref.shape[0]
    tiles = ext_ref[...].reshape(total // SUBLANES, SUBLANES, HALF)
    sub = lax.broadcasted_iota(jnp.int32, (1, SUBLANES, HALF), 1)
    for s in range(1, SUBLANES):
        rot = pltpu.roll(tiles, SUBLANES - s, axis=1)
        nxt = jnp.concatenate([rot[1:], rot[:1]], axis=0)
        shifted_ref[s - 1] = jnp.where(sub < SUBLANES - s, rot, nxt).reshape(total, HALF)


def _conv_chunks(ext_ref, shifted_ref, w_ref, out_ref, taps, first_row, chunks):
    for r in chunks:
        acc = None
        for k in range(taps):
            s = (first_row + k) % SUBLANES
            base = r * CONV_ROWS + first_row + k - s
            src = ext_ref if s == 0 else shifted_ref.at[s - 1]
            win = src[base:base + CONV_ROWS, :].reshape(CONV_ROWS // SUBLANES, SUBLANES, HALF)
            term = w_ref[k][None] * win
            acc = term if acc is None else acc + term
        out_ref[r * CONV_ROWS:(r + 1) * CONV_ROWS, :] = acc.reshape(CONV_ROWS, HALF)


def _mix_prompt_kernel(x_ref, k_ref, v_ref, norm_ref, win_ref, lng_ref, lnb_ref, ws_ref, bst_ref,
                       wa_ref, cbw_ref, cbb_ref, cblg_ref, cblb_ref, wb_ref, ccw_ref, wc_ref, wm_ref,
                       wg_ref, bg_ref, wo_ref, cast_gu_ref, cast_d_ref,
                       xo_ref, tailb_ref, tailc_ref, wgu_out_ref, wd_out_ref,
                       extb_ref, extc_ref, shiftb_ref, h_scr, yb_scr, merged_scr):
    _cast_blocks((cast_gu_ref, cast_d_ref), (wgu_out_ref, wd_out_ref))
    tm = x_ref.shape[0]
    j = pl.program_id(1)
    n_conv = tm // CONV_ROWS
    conv_split = (0, 6 * n_conv // 16, 11 * n_conv // 16, n_conv)

    def gate(i):
        z = _dot(h_scr[...], wg_ref[:, i * D_MODEL:(i + 1) * D_MODEL])
        return jax.nn.sigmoid(z + bg_ref[:, i * D_MODEL:(i + 1) * D_MODEL])

    def conv_b(part):
        _conv_chunks(extb_ref, shiftb_ref, cbw_ref, yb_scr, CONV_B_K, HALO_B - (CONV_B_K - 1),
                     range(conv_split[part], conv_split[part + 1]))

    @pl.when(j == 0)
    def _():
        extb_ref[0:HALO_B, :] = jnp.zeros((HALO_B, HALF), F32)
        extc_ref[0:HALO_C, :] = jnp.zeros((HALO_C, HALF), F32)

    h = _rms(x_ref[...], norm_ref[...]).astype(BF16)
    h_scr[...] = h

    zb = _dot(h, win_ref[:, 2 * HALF:4 * HALF])
    extb_ref[HALO_B:HALO_B + tm, :] = zb[:, :HALF] * jax.nn.sigmoid(zb[:, HALF:])
    _shifted_copies(extb_ref, shiftb_ref)

    conv_b(0)
    za = _gelu(_dot(h_scr[...], win_ref[:, 0:2 * HALF]))
    u = za[:, :HALF]
    v = _layer_norm(za[:, HALF:], lng_ref[...], lnb_ref[...]).astype(BF16)
    n_chunks = tm // CHUNK
    row = lax.broadcasted_iota(jnp.int32, (CHUNK, CHUNK), 0)
    col = lax.broadcasted_iota(jnp.int32, (CHUNK, CHUNK), 1)
    pieces = [[None] * GROUPS for _ in range(n_chunks)]
    for g in range(GROUPS):
        ws = jnp.where(row >= col, ws_ref[g], 0.0).astype(BF16)
        rhs = jnp.concatenate(
            [v[c * CHUNK:(c + 1) * CHUNK, g * LANES:(g + 1) * LANES] for c in range(n_chunks)], axis=1)
        res = _dot(ws, rhs) + bst_ref[:, g:g + 1]
        for c in range(n_chunks):
            pieces[c][g] = res[:, c * CHUNK:(c + 1) * CHUNK]
    mixed = jnp.concatenate([jnp.concatenate(p, axis=1) for p in pieces], axis=0)
    merged_scr[...] = gate(0) * _dot((u * mixed).astype(BF16), wa_ref[...])

    conv_b(1)
    zc = _dot(h_scr[...], win_ref[:, 4 * HALF:7 * HALF])
    extc_ref[HALO_C:HALO_C + tm, :] = zc[:, HALF:2 * HALF] * zc[:, 2 * HALF:]
    yc = _causal_conv(extc_ref, ccw_ref, CONV_C_K, HALO_C - (CONV_C_K - 1), tm)
    merged_scr[...] += gate(2) * _dot((zc[:, :HALF] * yc).astype(BF16), wc_ref[...])

    conv_b(2)
    q = _dot(h_scr[...], win_ref[:, 7 * HALF:8 * HALF])
    ones = jnp.ones((N_MEM, HEAD_DIM), BF16)
    heads = []
    for hd in range(HEADS):
        sl = slice(hd * HEAD_DIM, (hd + 1) * HEAD_DIM)
        kh = k_ref[pl.ds(hd, N_MEM, stride=HEADS), :].astype(BF16)
        vh = v_ref[pl.ds(hd, N_MEM, stride=HEADS), :].astype(BF16)
        s = lax.dot_general(q[:, sl].astype(BF16), kh, (((1,), (1,)), ((), ())),
                            preferred_element_type=F32) * ATTN_SCALE
        e = jnp.exp(s - jnp.max(s, axis=-1, keepdims=True)).astype(BF16)
        r = _dot(e, jnp.concatenate([vh, ones], axis=1))
        heads.append(r[:, :HEAD_DIM] / r[:, HEAD_DIM:])
    om = jnp.concatenate(heads, axis=1).astype(BF16)
    merged_scr[...] += gate(3) * _dot(om, wm_ref[...])

    yb = yb_scr[...] + cbb_ref[...]
    yb = jax.nn.silu(_layer_norm(yb, cblg_ref[...], cblb_ref[...]))
    merged = merged_scr[...] + gate(1) * _dot(yb.astype(BF16), wb_ref[...])
    xo_ref[...] = x_ref[...] + _dot(merged.astype(BF16), wo_ref[...])

    @pl.when(j == pl.num_programs(1) - 1)
    def _():
        tailb_ref[...] = extb_ref[tm + HALO_B - (CONV_B_K - 1):tm + HALO_B, :]
        tailc_ref[...] = extc_ref[tm + HALO_C - (CONV_C_K - 1):tm + HALO_C, :]

    extb_ref[0:HALO_B, :] = extb_ref[tm:tm + HALO_B, :]
    extc_ref[0:HALO_C, :] = extc_ref[tm:tm + HALO_C, :]


def _mix_prompt_call(x, kmem, vmem_, layer, w, mw, cast, *, batch, seq):
    tm = TM_MIX
    nj = seq // tm
    cast_in, cast_out, cast_shapes = _cast_specs(cast, layer, batch * nj, lambda b, j: b * nj + j)
    in_specs = [
        pl.BlockSpec((tm, D_MODEL), lambda b, j: (b * nj + j, 0)),
        pl.BlockSpec((None, None, N_MEM * HEADS, HEAD_DIM), lambda b, j: (layer, b, 0, 0)),
        pl.BlockSpec((None, None, N_MEM * HEADS, HEAD_DIM), lambda b, j: (layer, b, 0, 0)),
        _resident((1, D_MODEL), layer),
        _resident((D_MODEL, 8 * HALF)),
        _resident((1, HALF), layer),
        _resident((1, HALF), layer),
        _resident((GROUPS, CHUNK, CHUNK), layer),
        _resident((CHUNK, GROUPS), layer),
        _resident((HALF, D_MODEL)),
        _resident((CONV_B_K, SUBLANES, HALF), layer),
        _resident((1, HALF), layer),
        _resident((1, HALF), layer),
        _resident((1, HALF), layer),
        _resident((HALF, D_MODEL)),
        _resident((CONV_C_K, HALF), layer),
        _resident((HALF, D_MODEL)),
        _resident((HALF, D_MODEL)),
        _resident((D_MODEL, 4 * D_MODEL)),
        _resident((1, 4 * D_MODEL), layer),
        _resident((D_MODEL, D_MODEL)),
    ]
    out_specs = [
        pl.BlockSpec((tm, D_MODEL), lambda b, j: (b * nj + j, 0)),
        pl.BlockSpec((None, CONV_B_K - 1, HALF), lambda b, j: (b, 0, 0)),
        pl.BlockSpec((None, CONV_C_K - 1, HALF), lambda b, j: (b, 0, 0)),
    ]
    out_shape = [
        jax.ShapeDtypeStruct((batch * seq, D_MODEL), F32),
        jax.ShapeDtypeStruct((batch, CONV_B_K - 1, HALF), F32),
        jax.ShapeDtypeStruct((batch, CONV_C_K - 1, HALF), F32),
    ]
    return pl.pallas_call(
        _mix_prompt_kernel,
        grid=(batch, nj),
        in_specs=in_specs + cast_in,
        out_specs=out_specs + cast_out,
        out_shape=out_shape + cast_shapes,
        scratch_shapes=[pltpu.VMEM((HALO_B + tm, HALF), F32), pltpu.VMEM((HALO_C + tm, HALF), F32),
                        pltpu.VMEM((SUBLANES - 1, HALO_B + tm, HALF), F32),
                        pltpu.VMEM((tm, D_MODEL), BF16),
                        pltpu.VMEM((tm, HALF), F32),
                        pltpu.VMEM((tm, D_MODEL), F32)],
        compiler_params=pltpu.CompilerParams(
            dimension_semantics=("arbitrary", "arbitrary"), vmem_limit_bytes=VMEM_LIMIT),
        name=f"mix_prompt_{layer}",
    )(x, kmem, vmem_, w["mix_norm"], mw["w_in"], w["gmlp_ln_g"], w["gmlp_ln_b"], w["gmlp_w_s"],
      w["gmlp_b_s_t"], mw["gmlp_w_out"], w["conv_b_w8"], w["conv_b_bias"], w["conv_b_ln_g"],
      w["conv_b_ln_b"], mw["conv_b_w_out"], w["conv_c_w"], mw["conv_c_w_out"], mw["mem_w_out"],
      mw["w_branch_gate"], w["b_branch_gate"], mw["w_o"], *cast)


def _mix_sample_kernel(x_ref, k_ref, v_ref, stb_ref, stc_ref, norm_ref, win_ref, lng_ref, lnb_ref,
                       ws0_ref, bs0_ref, wa_ref, cbw_ref, cbb_ref, cblg_ref, cblb_ref, wb_ref, ccw_ref,
                       wc_ref, wm_ref, wg_ref, bg_ref, wo_ref,
                       xo_ref, gv_ref, nstb_ref, nstc_ref,
                       h_scr, pa_scr, xinb_scr, gatec_scr, xinc_scr, q_scr, yb_scr, yc_scr, o_scr):
    i = pl.program_id(0)
    sb = stb_ref.shape[1]
    r0 = pl.multiple_of(i * sb, sb)

    @pl.when(i == 0)
    def _():
        h = _rms(x_ref[...], norm_ref[...]).astype(BF16)
        h_scr[...] = h
        za = _gelu(_dot(h, win_ref[:, 0:2 * HALF]))
        v = _layer_norm(za[:, HALF:], lng_ref[...], lnb_ref[...])
        gv_ref[...] = v
        pa_scr[...] = za[:, :HALF] * (ws0_ref[...] * v + bs0_ref[...])
        zb = _dot(h, win_ref[:, 2 * HALF:4 * HALF])
        xinb_scr[...] = zb[:, :HALF] * jax.nn.sigmoid(zb[:, HALF:])
        zc = _dot(h, win_ref[:, 4 * HALF:7 * HALF])
        gatec_scr[...] = zc[:, :HALF]
        xinc_scr[...] = zc[:, HALF:2 * HALF] * zc[:, 2 * HALF:]
        q = _dot(h, win_ref[:, 7 * HALF:8 * HALF])
        for hd in range(HEADS):
            qh = q[:, hd * HEAD_DIM:(hd + 1) * HEAD_DIM]
            q_scr[:, hd, :] = qh
            q_scr[:, hd + HEADS, :] = qh

    xin_b = xinb_scr[pl.ds(r0, sb), :]
    acc = cbw_ref[CONV_B_K - 1:CONV_B_K, :] * xin_b
    for k in range(CONV_B_K - 1):
        acc = acc + cbw_ref[k:k + 1, :] * stb_ref[k]
    yb_scr[pl.ds(r0, sb), :] = acc
    nstb_ref[0:CONV_B_K - 2] = stb_ref[1:CONV_B_K - 1]
    nstb_ref[CONV_B_K - 2] = xin_b

    xin_c = xinc_scr[pl.ds(r0, sb), :]
    yc_scr[pl.ds(r0, sb), :] = (ccw_ref[0:1, :] * stc_ref[0] + ccw_ref[1:2, :] * stc_ref[1]
                                + ccw_ref[2:3, :] * xin_c)
    nstc_ref[0] = stc_ref[1]
    nstc_ref[1] = xin_c

    ones = jnp.ones((HEAD_DIM, LANES), BF16)
    n_tiles = N_MEM * HEADS // SUBLANES
    for s in range(sb):
        q8 = q_scr[r0 + s]
        prod = (k_ref[s] * q8[None]).reshape(n_tiles * SUBLANES, HEAD_DIM)
        sc = (_dot(prod.astype(BF16), ones) * ATTN_SCALE).reshape(n_tiles, SUBLANES, LANES)
        mx = jnp.max(sc, axis=0)
        mx = jnp.maximum(mx, pltpu.roll(mx, HEADS, axis=0))
        e = jnp.exp(sc - mx[None])
        num = jnp.sum(e * v_ref[s], axis=0)
        den = jnp.sum(e, axis=0)
        num = num + pltpu.roll(num, HEADS, axis=0)
        den = den + pltpu.roll(den, HEADS, axis=0)
        o_scr[r0 + s] = num / den

    @pl.when(i == pl.num_programs(0) - 1)
    def _():
        h = h_scr[...]

        def gate(n):
            z = _dot(h, wg_ref[:, n * D_MODEL:(n + 1) * D_MODEL]) + bg_ref[:, n * D_MODEL:(n + 1) * D_MODEL]
            return jax.nn.sigmoid(z)

        merged = gate(0) * _dot(pa_scr[...].astype(BF16), wa_ref[...])
        yb = jax.nn.silu(_layer_norm(yb_scr[...] + cbb_ref[...], cblg_ref[...], cblb_ref[...]))
        merged = merged + gate(1) * _dot(yb.astype(BF16), wb_ref[...])
        merged = merged + gate(2) * _dot((gatec_scr[...] * yc_scr[...]).astype(BF16), wc_ref[...])
        om = None
        for hd in range(HEADS):
            part = _dot(o_scr[:, hd, :].astype(BF16), wm_ref[hd * HEAD_DIM:(hd + 1) * HEAD_DIM, :])
            om = part if om is None else om + part
        merged = merged + gate(3) * om
        xo_ref[...] = x_ref[...] + _dot(merged.astype(BF16), wo_ref[...])


def _mix_sample_call(x, kc, vc, stb, stc, layer, w, mw):
    n = x.shape[0]
    sb = SAMPLE_BLOCK
    n_tiles = N_MEM * HEADS // SUBLANES
    const = lambda shape: pl.BlockSpec(shape, lambda i: (0,) * len(shape))
    in_specs = [
        const((n, D_MODEL)),
        pl.BlockSpec((None, sb, n_tiles, SUBLANES, HEAD_DIM), lambda i: (layer, i, 0, 0, 0)),
        pl.BlockSpec((None, sb, n_tiles, SUBLANES, HEAD_DIM), lambda i: (layer, i, 0, 0, 0)),
        pl.BlockSpec((None, CONV_B_K - 1, sb, HALF), lambda i: (layer, 0, i, 0)),
        pl.BlockSpec((None, CONV_C_K - 1, sb, HALF), lambda i: (layer, 0, i, 0)),
        _resident((1, D_MODEL), layer),
        _resident((D_MODEL, 8 * HALF)),
        _resident((1, HALF), layer),
        _resident((1, HALF), layer),
        _resident((1, HALF), layer),
        _resident((1, HALF), layer),
        _resident((HALF, D_MODEL)),
        _resident((CONV_B_K, HALF), layer),
        _resident((1, HALF), layer),
        _resident((1, HALF), layer),
        _resident((1, HALF), layer),
        _resident((HALF, D_MODEL)),
        _resident((CONV_C_K, HALF), layer),
        _resident((HALF, D_MODEL)),
        _resident((HALF, D_MODEL)),
        _resident((D_MODEL, 4 * D_MODEL)),
        _resident((1, 4 * D_MODEL), layer),
        _resident((D_MODEL, D_MODEL)),
    ]
    out_specs = [
        const((n, D_MODEL)),
        const((n, HALF)),
        pl.BlockSpec((CONV_B_K - 1, sb, HALF), lambda i: (0, i, 0)),
        pl.BlockSpec((CONV_C_K - 1, sb, HALF), lambda i: (0, i, 0)),
    ]
    out_shape = [
        jax.ShapeDtypeStruct((n, D_MODEL), F32),
        jax.ShapeDtypeStruct((n, HALF), F32),
        jax.ShapeDtypeStruct((CONV_B_K - 1, n, HALF), F32),
        jax.ShapeDtypeStruct((CONV_C_K - 1, n, HALF), F32),
    ]
    scratch = [
        pltpu.VMEM((n, D_MODEL), BF16),
        pltpu.VMEM((n, HALF), F32),
        pltpu.VMEM((n, HALF), F32),
        pltpu.VMEM((n, HALF), F32),
        pltpu.VMEM((n, HALF), F32),
        pltpu.VMEM((n, SUBLANES, HEAD_DIM), F32),
        pltpu.VMEM((n, HALF), F32),
        pltpu.VMEM((n, HALF), F32),
        pltpu.VMEM((n, SUBLANES, HEAD_DIM), F32),
    ]
    return pl.pallas_call(
        _mix_sample_kernel,
        grid=(n // sb,),
        in_specs=in_specs,
        out_specs=out_specs,
        out_shape=out_shape,
        scratch_shapes=scratch,
        compiler_params=pltpu.CompilerParams(
            dimension_semantics=("arbitrary",), vmem_limit_bytes=VMEM_LIMIT),
        name=f"mix_sample_{layer}",
    )(x, kc, vc, stb, stc, w["mix_norm"], mw["w_in"], w["gmlp_ln_g"], w["gmlp_ln_b"], w["gmlp_ws0"],
      w["gmlp_bs0"], mw["gmlp_w_out"], w["conv_b_w"], w["conv_b_bias"], w["conv_b_ln_g"],
      w["conv_b_ln_b"], mw["conv_b_w_out"], w["conv_c_w"], mw["conv_c_w_out"], mw["mem_w_out"],
      mw["w_branch_gate"], w["b_branch_gate"], mw["w_o"])


def kernel(x_prompt, x_sample, mem_prompt, state_conv_b, state_conv_c, cache_mem_k, cache_mem_v,
           ffn1_norm, ffn1_w_gate_up, ffn1_w_down, mix_norm, w_in,
           gmlp_ln_g, gmlp_ln_b, gmlp_w_s, gmlp_b_s, gmlp_w_out,
           conv_b_w, conv_b_bias, conv_b_ln_g, conv_b_ln_b, conv_b_w_out,
           conv_c_w, conv_c_w_out, mem_norm, mem_w_k, mem_w_v, mem_w_out,
           w_branch_gate, b_branch_gate, w_o, ffn2_norm, ffn2_w_gate_up, ffn2_w_down, final_norm):
    batch, seq, _ = x_prompt.shape
    n_sample = x_sample.shape[0]
    row = lambda a: a.reshape(DEPTH, 1, a.shape[-1])

    w = {
        "mix_norm": row(mix_norm),
        "gmlp_ln_g": row(gmlp_ln_g), "gmlp_ln_b": row(gmlp_ln_b),
        "gmlp_w_s": gmlp_w_s, "gmlp_b_s_t": jnp.swapaxes(gmlp_b_s, 1, 2),
        "gmlp_ws0": row(jnp.repeat(gmlp_w_s[:, :, 0, 0], LANES, axis=1)),
        "gmlp_bs0": row(jnp.repeat(gmlp_b_s[:, :, 0], LANES, axis=1)),
        "conv_b_w": conv_b_w, "conv_b_bias": row(conv_b_bias),
        "conv_b_w8": jnp.broadcast_to(conv_b_w[:, :, None, :], (DEPTH, CONV_B_K, SUBLANES, HALF)),
        "conv_b_ln_g": row(conv_b_ln_g), "conv_b_ln_b": row(conv_b_ln_b),
        "conv_c_w": conv_c_w,
        "b_branch_gate": row(b_branch_gate),
    }
    mix_names = ("w_in", "w_branch_gate", "w_o", "gmlp_w_out", "conv_b_w_out", "conv_c_w_out", "mem_w_out")
    mix_f32 = [w_in, w_branch_gate, w_o, gmlp_w_out, conv_b_w_out, conv_c_w_out, mem_w_out]
    ffn1_f32 = [ffn1_w_gate_up, ffn1_w_down]
    ffn2_f32 = [ffn2_w_gate_up, ffn2_w_down]
    ffn1_norm, ffn2_norm = row(ffn1_norm), row(ffn2_norm)
    final = final_norm.reshape(1, D_MODEL)

    kmem, vmem_ = _memkv_call(mem_prompt.reshape(batch * N_MEM, D_MODEL), row(mem_norm),
                              mem_w_k.astype(BF16), mem_w_v.astype(BF16))

    n_tiles = N_MEM * HEADS // SUBLANES
    kc = cache_mem_k.reshape(DEPTH, n_sample, n_tiles, SUBLANES, HEAD_DIM)
    vc = cache_mem_v.reshape(DEPTH, n_sample, n_tiles, SUBLANES, HEAD_DIM)

    stb = jnp.swapaxes(state_conv_b, 1, 2)
    stc = jnp.swapaxes(state_conv_c, 1, 2)

    xp = x_prompt.reshape(batch * seq, D_MODEL)
    xs = x_sample.reshape(n_sample, D_MODEL)
    cb_p, cb_s, cc_p, cc_s, gv_s = [], [], [], [], []
    ffn1_w = [a[0].astype(BF16) for a in ffn1_f32]
    for l in range(DEPTH):
        last = l == DEPTH - 1
        xp, xs, *mix_w = _ffn_call(xp, xs, l, ffn1_norm, *ffn1_w, None, mix_f32, l, name=f"ffn1_{l}")
        mw = dict(zip(mix_names, mix_w))
        xp, tb, tc, *ffn2_w = _mix_prompt_call(xp, kmem, vmem_, l, w, mw, ffn2_f32, batch=batch, seq=seq)
        xs, gv, nb, nc = _mix_sample_call(xs, kc, vc, stb, stc, l, w, mw)
        xp, xs, *ffn1_w = _ffn_call(xp, xs, l, ffn2_norm, *ffn2_w, final if last else None,
                                    [] if last else ffn1_f32, l + 1, name=f"ffn2_{l}")
        cb_p.append(tb); cc_p.append(tc); cb_s.append(nb); cc_s.append(nc); gv_s.append(gv)

    kv_shape = (DEPTH, batch, N_MEM, HEADS, HEAD_DIM)
    return (xp.reshape(batch, seq, D_MODEL), xs.reshape(n_sample, 1, D_MODEL),
            kmem.reshape(kv_shape), vmem_.reshape(kv_shape),
            jnp.stack(cb_p), jnp.swapaxes(jnp.stack(cb_s), 1, 2),
            jnp.stack(cc_p), jnp.swapaxes(jnp.stack(cc_s), 1, 2),
            jnp.stack(gv_s).reshape(DEPTH, n_sample, 1, HALF))
```
